```python
import math
import jax, jax.numpy as jnp
from jax import lax
import numpy as np

D_MODEL = 1024
BATCH = 8
SEQ = 2048
DEPTH = 4
DEC_BATCH = 128
DEC_SEQ = 4
PAST_LEN = 8192
PAGE_SIZE = 128

N_MLA = (DEPTH + 1) // 2
N_DIFF = DEPTH // 2
MIX_W = D_MODEL
MEM_LEN = 256
MEM_HEADS = 4
MEM_HD = 64
MEM_W = MEM_HEADS * MEM_HD
SELF_W = MIX_W - MEM_W
MLA_HEADS = 12
MLA_NOPE = 64
MLA_ROPE = 32
MLA_VD = 64
Q_LORA = 256
KV_LORA = 128
MLA_ROW = KV_LORA + MLA_ROPE
MLA_SELF_IN = Q_LORA + KV_LORA + MLA_ROPE
MLA_IN_W = MLA_SELF_IN + MEM_W + MIX_W
ROPE_THETA = 10000.0
DIF_HEADS = 6
DIF_KV_HEADS = 3
DIF_HD = 64
DIF_VD = 2 * DIF_HD
DIF_QW = DIF_HEADS * DIF_VD
DIF_KW = DIF_KV_HEADS * DIF_VD
DIF_SELF_IN = DIF_QW + 2 * DIF_KW
DIF_IN_W = DIF_SELF_IN + MEM_W + MIX_W
Q_BLOCK = 128
RMS_EPS = 1e-6

kernel_name = 'hybrid_mla_diffattn_memxattn_step'


def rms_norm(x, g):
    xf = x.astype(jnp.float32)
    y = xf * lax.rsqrt(jnp.mean(xf * xf, axis=-1, keepdims=True) + RMS_EPS)
    return (y * g.astype(jnp.float32)).astype(x.dtype)


def rope(x, pos):
    half = x.shape[-1] // 2
    inv_freq = ROPE_THETA ** (-jnp.arange(half, dtype=jnp.float32) / half)
    ang = pos.astype(jnp.float32)[:, None] * inv_freq
    ang = ang.reshape((ang.shape[0],) + (1,) * (x.ndim - 3) + (half,))
    c, s = jnp.cos(ang), jnp.sin(ang)
    xf = x.astype(jnp.float32)
    x1, x2 = xf[..., :half], xf[..., half:]
    return jnp.concatenate([x1 * c - x2 * s, x2 * c + x1 * s], axis=-1).astype(x.dtype)


def alibi_slopes(n):
    return 2.0 ** (-8.0 * jnp.arange(1, n + 1, dtype=jnp.float32) / n)


def causal_mask(q_pos, k_pos):
    return k_pos[None, :] <= q_pos[:, None]


def sweep_queries(fn, qs, q_pos):
    t = q_pos.shape[0]
    nb = t // Q_BLOCK
    def split(a):
        return jnp.moveaxis(a.reshape((a.shape[0], nb, Q_BLOCK) + a.shape[2:]), 1, 0)
    out = lax.map(lambda args: fn(*args), tuple(split(a) for a in qs) + (q_pos.reshape(nb, Q_BLOCK),))
    out = jnp.moveaxis(out, 0, 1)
    return out.reshape((out.shape[0], t) + out.shape[3:])


def mla_queries_and_rows(z, pos, q_a_g, w_q_up, kv_a_g, qn_g, qr_g, kr_g):
    b, t = z.shape[:2]
    z_q = z[..., :Q_LORA]
    z_c = z[..., Q_LORA:Q_LORA + KV_LORA]
    z_r = z[..., Q_LORA + KV_LORA:MLA_SELF_IN]
    q = (rms_norm(z_q, q_a_g) @ w_q_up).reshape(b, t, MLA_HEADS, MLA_NOPE + MLA_ROPE)
    q_nope = rms_norm(q[..., :MLA_NOPE], qn_g)
    q_rope = rope(rms_norm(q[..., MLA_NOPE:], qr_g), pos)
    rows = jnp.concatenate([rms_norm(z_c, kv_a_g), rope(rms_norm(z_r, kr_g), pos)], axis=-1)
    return q_nope, q_rope, rows


def mla_expand(rows, w_uk, w_uv, kn_g):
    b, s = rows.shape[:2]
    lat = rows[..., :KV_LORA]
    k_nope = rms_norm((lat @ w_uk).reshape(b, s, MLA_HEADS, MLA_NOPE), kn_g)
    v = (lat @ w_uv).reshape(b, s, MLA_HEADS, MLA_VD)
    return k_nope, rows[..., KV_LORA:], v


def mla_core(q_nope, q_rope, q_pos, k_nope, k_rope, v, k_pos):
    scale = (MLA_NOPE + MLA_ROPE) ** -0.5
    s = (jnp.einsum('bqhd,bshd->bhqs', q_nope, k_nope, preferred_element_type=jnp.float32)
         + jnp.einsum('bqhr,bsr->bhqs', q_rope, k_rope, preferred_element_type=jnp.float32)) * scale
    s = jnp.where(causal_mask(q_pos, k_pos), s, -jnp.inf)
    p = jax.nn.softmax(s, axis=-1)
    return jnp.einsum('bhqs,bshd->bqhd', p.astype(v.dtype), v)


def diff_qkv(z, q_g, k_g):
    b, t = z.shape[:2]
    q = rms_norm(z[..., :DIF_QW].reshape(b, t, DIF_HEADS, 2, DIF_HD), q_g)
    k = rms_norm(z[..., DIF_QW:DIF_QW + DIF_KW].reshape(b, t, DIF_KV_HEADS, 2, DIF_HD), k_g)
    v = z[..., DIF_QW + DIF_KW:DIF_SELF_IN].reshape(b, t, DIF_KV_HEADS, DIF_VD)
    return q, k.reshape(b, t, DIF_KV_HEADS, DIF_VD), v


def diff_lambda(lp, lam_init):
    lp = lp.astype(jnp.float32)
    return jnp.exp(jnp.sum(lp[0] * lp[1])) - jnp.exp(jnp.sum(lp[2] * lp[3])) + lam_init


def diff_core(q, k, v, q_pos, k_pos, lam, slopes):
    b, tq = q.shape[:2]
    g = DIF_HEADS // DIF_KV_HEADS
    qg = q.reshape(b, tq, DIF_KV_HEADS, g, 2, DIF_HD)
    kk = k.reshape(k.shape[0], k.shape[1], DIF_KV_HEADS, 2, DIF_HD)
    s = jnp.einsum('bqkgcd,bskcd->cbkgqs', qg, kk, preferred_element_type=jnp.float32) * DIF_HD ** -0.5
    dist = (q_pos[:, None] - k_pos[None, :]).astype(jnp.float32)
    bias = -slopes.reshape(DIF_KV_HEADS, g)[:, :, None, None] * dist
    s = jnp.where(causal_mask(q_pos, k_pos), s + bias, -jnp.inf)
    p = jax.nn.softmax(s, axis=-1)
    a = p[0] - lam * p[1]
    o = jnp.einsum('bkgqs,bskd->bqkgd', a.astype(v.dtype), v)
    return o.reshape(b, tq, DIF_HEADS, DIF_VD)


def diff_out(o, subln_g, lam_init):
    b, t = o.shape[:2]
    return (rms_norm(o, subln_g) * (1.0 - lam_init)).reshape(b, t, SELF_W)


def mem_kv(mem, g, w, ck_g):
    b, m = mem.shape[:2]
    kv = rms_norm(mem, g) @ w
    k = rms_norm(kv[..., :MEM_W].reshape(b, m, MEM_HEADS, MEM_HD), ck_g)
    v = kv[..., MEM_W:].reshape(b, m, MEM_HEADS, MEM_HD)
    return k, v


def mem_cross(cq, mk, mv, cq_g):
    b, t = cq.shape[:2]
    q = rms_norm(cq.reshape(b, t, MEM_HEADS, MEM_HD), cq_g)
    s = jnp.einsum('bqhd,bmhd->bhqm', q, mk, preferred_element_type=jnp.float32) * MEM_HD ** -0.5
    p = jax.nn.softmax(s, axis=-1)
    return jnp.einsum('bhqm,bmhd->bqhd', p.astype(mv.dtype), mv).reshape(b, t, MEM_W)


def gated_merge(self_out, cross_out, z, w_out):
    gate = z[..., -MIX_W:]
    return (jnp.concatenate([self_out, cross_out], axis=-1) * jax.nn.silu(gate)) @ w_out


def setup_inputs(seed: int = 0) -> dict:
    key = jax.random.key(seed)
    ks = iter(jax.random.split(key, 40))
    f32 = jnp.float32
    def nrm(shape, scale=1.0):
        return scale * jax.random.normal(next(ks), shape, f32)
    def gain(shape):
        return 1.0 + 0.02 * jax.random.normal(next(ks), shape, f32)
    n_pages = PAST_LEN // PAGE_SIZE
    n_used = DEC_BATCH * n_pages
    n_pool = (n_used * 5) // 4
    x_prompt = nrm((BATCH, SEQ, D_MODEL))
    x_sample = nrm((DEC_BATCH, DEC_SEQ, D_MODEL))
    cache_mla_kv = nrm((n_pool, N_MLA, PAGE_SIZE, MLA_ROW))
    cache_diff_k = nrm((n_pool, N_DIFF, PAGE_SIZE, DIF_KV_HEADS, DIF_VD))
    cache_diff_v = nrm((n_pool, N_DIFF, PAGE_SIZE, DIF_KV_HEADS, DIF_VD))
    cache_mem_k = nrm((DEC_BATCH, DEPTH, MEM_LEN, MEM_HEADS, MEM_HD))
    cache_mem_v = nrm((DEC_BATCH, DEPTH, MEM_LEN, MEM_HEADS, MEM_HD))
    page_table = jax.random.permutation(next(ks), n_pool)[:n_used].reshape(DEC_BATCH, n_pages).astype(jnp.int32)
    mem_prompt = nrm((BATCH, MEM_LEN, D_MODEL))
    return {
        'x_prompt': x_prompt,
        'x_sample': x_sample,
        'cache_mla_kv': cache_mla_kv,
        'cache_diff_k': cache_diff_k,
        'cache_diff_v': cache_diff_v,
        'cache_mem_k': cache_mem_k,
        'cache_mem_v': cache_mem_v,
        'page_table': page_table,
        'mem_prompt': mem_prompt,
        'norm_g': gain((DEPTH, D_MODEL)),
        'w_out': nrm((DEPTH, MIX_W, D_MODEL), MIX_W ** -0.5),
        'mem_norm_g': gain((DEPTH, D_MODEL)),
        'w_mem_kv': nrm((DEPTH, D_MODEL, 2 * MEM_W), D_MODEL ** -0.5),
        'cq_norm_g': gain((DEPTH, MEM_HD)),
        'ck_norm_g': gain((DEPTH, MEM_HD)),
        'mla_w_in': nrm((N_MLA, D_MODEL, MLA_IN_W), D_MODEL ** -0.5),
        'mla_q_a_norm_g': gain((N_MLA, Q_LORA)),
        'mla_w_q_up': nrm((N_MLA, Q_LORA, MLA_HEADS * (MLA_NOPE + MLA_ROPE)), Q_LORA ** -0.5),
        'mla_kv_a_norm_g': gain((N_MLA, KV_LORA)),
        'mla_w_uk': nrm((N_MLA, KV_LORA, MLA_HEADS * MLA_NOPE), KV_LORA ** -0.5),
        'mla_w_uv': nrm((N_MLA, KV_LORA, MLA_HEADS * MLA_VD), KV_LORA ** -0.5),
        'mla_qn_g': gain((N_MLA, MLA_NOPE)),
        'mla_kn_g': gain((N_MLA, MLA_NOPE)),
        'mla_qr_g': gain((N_MLA, MLA_ROPE)),
        'mla_kr_g': gain((N_MLA, MLA_ROPE)),
        'dif_w_in': nrm((N_DIFF, D_MODEL, DIF_IN_W), D_MODEL ** -0.5),
        'dif_q_g': gain((N_DIFF, DIF_HD)),
        'dif_k_g': gain((N_DIFF, DIF_HD)),
        'dif_lambda': nrm((N_DIFF, 4, DIF_HD), 0.1),
        'dif_subln_g': gain((N_DIFF, DIF_VD)),
    }


def reference(x_prompt, x_sample, cache_mla_kv, cache_diff_k, cache_diff_v, cache_mem_k, cache_mem_v,
              page_table, mem_prompt, norm_g, w_out, mem_norm_g, w_mem_kv, cq_norm_g, ck_norm_g,
              mla_w_in, mla_q_a_norm_g, mla_w_q_up, mla_kv_a_norm_g, mla_w_uk, mla_w_uv,
              mla_qn_g, mla_kn_g, mla_qr_g, mla_kr_g,
              dif_w_in, dif_q_g, dif_k_g, dif_lambda, dif_subln_g):
    b, t = x_prompt.shape[:2]
    db, ds = x_sample.shape[:2]
    past = page_table.shape[1] * PAGE_SIZE
    pos_p = jnp.arange(t, dtype=jnp.int32)
    pos_s = past + jnp.arange(ds, dtype=jnp.int32)
    kpos_s = jnp.arange(past + ds, dtype=jnp.int32)
    slopes = alibi_slopes(DIF_HEADS)

    xp, xs = x_prompt, x_sample
    mla_rows_p, mla_rows_s = [], []
    dk_p, dv_p, dk_s, dv_s = [], [], [], []
    mk_out, mv_out = [], []

    for i in range(DEPTH):
        j = i // 2
        hp = rms_norm(xp, norm_g[i])
        hs = rms_norm(xs, norm_g[i])
        mk_p, mv_p = mem_kv(mem_prompt, mem_norm_g[i], w_mem_kv[i], ck_norm_g[i])
        mk_out.append(mk_p)
        mv_out.append(mv_p)

        if i % 2 == 0:
            zp = hp @ mla_w_in[j]
            zs = hs @ mla_w_in[j]
            proj = (mla_q_a_norm_g[j], mla_w_q_up[j], mla_kv_a_norm_g[j], mla_qn_g[j], mla_qr_g[j], mla_kr_g[j])
            expand = (mla_w_uk[j], mla_w_uv[j], mla_kn_g[j])

            qn_p, qr_p, rows_p = mla_queries_and_rows(zp, pos_p, *proj)
            kn_p, kr_p, v_p = mla_expand(rows_p, *expand)
            self_p = sweep_queries(
                lambda qn, qr, qp: mla_core(qn, qr, qp, kn_p, kr_p, v_p, pos_p),
                (qn_p, qr_p), pos_p).reshape(b, t, SELF_W)

            qn_s, qr_s, rows_s = mla_queries_and_rows(zs, pos_s, *proj)
            def mla_one(args):
                qn, qr, rows_new, pt = args
                past_rows = cache_mla_kv[pt, j].reshape(-1, MLA_ROW)
                rows = jnp.concatenate([past_rows, rows_new.astype(past_rows.dtype)], axis=0)[None]
                kn, kr, v = mla_expand(rows, *expand)
                return mla_core(qn[None], qr[None], pos_s, kn, kr, v, kpos_s)[0]
            self_s = lax.map(mla_one, (qn_s, qr_s, rows_s, page_table)).reshape(db, ds, SELF_W)

            mla_rows_p.append(rows_p)
            mla_rows_s.append(rows_s)
        else:
            lam_init = 0.8 - 0.6 * math.exp(-0.3 * i)
            lam = diff_lambda(dif_lambda[j], lam_init)
            zp = hp @ dif_w_in[j]
            zs = hs @ dif_w_in[j]

            q_p, k_p, v_p = diff_qkv(zp, dif_q_g[j], dif_k_g[j])
            o_p = sweep_queries(
                lambda q, qp: diff_core(q, k_p, v_p, qp, pos_p, lam, slopes), (q_p,), pos_p)
            self_p = diff_out(o_p, dif_subln_g[j], lam_init)

            q_s, k_s, v_s = diff_qkv(zs, dif_q_g[j], dif_k_g[j])
            def diff_one(args):
                q, k_new, v_new, pt = args
                kp = cache_diff_k[pt, j].reshape(-1, DIF_KV_HEADS, DIF_VD)
                vp = cache_diff_v[pt, j].reshape(-1, DIF_KV_HEADS, DIF_VD)
                kk = jnp.concatenate([kp, k_new.astype(kp.dtype)], axis=0)[None]
                vv = jnp.concatenate([vp, v_new.astype(vp.dtype)], axis=0)[None]
                return diff_core(q[None], kk, vv, pos_s, kpos_s, lam, slopes)[0]
            o_s = lax.map(diff_one, (q_s, k_s, v_s, page_table))
            self_s = diff_out(o_s, dif_subln_g[j], lam_init)

            dk_p.append(k_p)
            dv_p.append(v_p)
            dk_s.append(k_s)
            dv_s.append(v_s)

        cross_p = mem_cross(zp[..., -(MIX_W + MEM_W):-MIX_W], mk_p, mv_p, cq_norm_g[i])
        cross_s = mem_cross(zs[..., -(MIX_W + MEM_W):-MIX_W], cache_mem_k[:, i], cache_mem_v[:, i], cq_norm_g[i])
        xp = xp + gated_merge(self_p, cross_p, zp, w_out[i])
        xs = xs + gated_merge(self_s, cross_s, zs, w_out[i])

    mla_kv_prompt = jnp.stack(mla_rows_p, axis=1)
    diff_k_prompt = jnp.stack(dk_p, axis=1)
    diff_v_prompt = jnp.stack(dv_p, axis=1)
    mem_k_prompt = jnp.stack(mk_out, axis=1)
    mem_v_prompt = jnp.stack(mv_out, axis=1)
    mla_kv_sample = jnp.stack(mla_rows_s, axis=1)
    diff_k_sample = jnp.stack(dk_s, axis=1)
    diff_v_sample = jnp.stack(dv_s, axis=1)
    return (xp, xs, mla_kv_prompt, diff_k_prompt, diff_v_prompt, mem_k_prompt, mem_v_prompt,
            mla_kv_sample, diff_k_sample, diff_v_sample)
```

```python
import functools
import math

import jax
import jax.numpy as jnp
from jax import lax
from jax.experimental import pallas as pl
from jax.experimental.pallas import tpu as pltpu

F32 = jnp.float32
BF16 = jnp.bfloat16

LANES = 128
SUBLANES = 8
VMEM_LIMIT = 56 * 1024 * 1024

D_MODEL = 1024
PAGE = 128
MEM_LEN = 256
MEM_HEADS = 4
MEM_HD = 64
MEM_W = MEM_HEADS * MEM_HD
MIX_W = D_MODEL
SELF_W = MIX_W - MEM_W
MLA_HEADS = 12
MLA_NOPE = 64
MLA_ROPE = 32
MLA_VD = 64
Q_LORA = 256
KV_LORA = 128
MLA_ROW = KV_LORA + MLA_ROPE
ROW_PAD = 2 * LANES
ROPE_THETA = 10000.0
DIF_HEADS = 6
DIF_KV_HEADS = 3
DIF_HD = 64
DIF_VD = 2 * DIF_HD
DIF_QW = DIF_HEADS * DIF_VD
DIF_KW = DIF_KV_HEADS * DIF_VD
EPS = 1e-6
NEG_INF = float("-inf")

PROJ_TM = 256
FLASH_T = 256
PAGES_PER_STEP = 4


def _dot(a, b):
    return jnp.dot(a.astype(BF16), b.astype(BF16), preferred_element_type=F32)


def _dot_nt(a, b):
    return lax.dot_general(a.astype(BF16), b.astype(BF16), (((1,), (1,)), ((), ())),
                           preferred_element_type=F32)


def _iota(shape, dim):
    return lax.broadcasted_iota(jnp.int32, shape, dim)


def _rms(x):
    return x * lax.rsqrt(jnp.mean(x * x, axis=-1, keepdims=True) + EPS)


def _seg_rsqrt(sq, lane, lo, hi):
    ss = jnp.sum(jnp.where((lane >= lo) & (lane < hi), sq, 0.0), axis=-1, keepdims=True)
    return lax.rsqrt(ss * (1.0 / (hi - lo)) + EPS)


def _norm_halves(blk, lane):
    sq = blk * blk
    r0 = _seg_rsqrt(sq, lane, 0, 64)
    r1 = _seg_rsqrt(sq, lane, 64, 128)
    return blk * jnp.where(lane < 64, r0, r1)


def _silu(x):
    return x / (1.0 + jnp.exp(-x))


def _params(sem):
    return pltpu.CompilerParams(dimension_semantics=sem, vmem_limit_bytes=VMEM_LIMIT)


def _cross_query_and_gate(z_cq, z_gate, cqg, cqn_ref, gs_ref, lane):
    for p in range(MEM_W // LANES):
        blk = z_cq[:, p * LANES:(p + 1) * LANES]
        cqn_ref[:, p * LANES:(p + 1) * LANES] = _norm_halves(blk, lane) * cqg
    gs_ref[...] = _silu(z_gate)


def _mla_proj_kernel(sample, x_ref, g_ref, win_ref, qag_ref, wq_ref, kvg_ref,
                     qa_ref, qb_ref, ka_ref, kb_ref, cqg_ref, *rest):
    if sample:
        wabs_ref, qabs_ref, qrope_ref, rows_ref, cqn_ref, gs_ref = rest
    else:
        wuk_ref, kng_ref, wuv_ref, qhat_ref, khat_ref, v_ref, rows_ref, cqn_ref, gs_ref = rest
    tm = x_ref.shape[0]
    lane = _iota((tm, LANES), 1)

    h = _rms(x_ref[...]) * g_ref[...]
    z = _dot(h, win_ref[...])
    z_q = z[:, 0:256]
    z_c = z[:, 256:384]
    z_r = z[:, 384:512]
    _cross_query_and_gate(z[:, 512:768], z[:, 768:1792], cqg_ref[...], cqn_ref, gs_ref, lane)

    q = _dot(_rms(z_q) * qag_ref[...], wq_ref[...])
    qa = qa_ref[...]
    qb = qb_ref[...]
    for hd in range(MLA_HEADS):
        blk = q[:, hd * LANES:(hd + 1) * LANES]
        sq = blk * blk
        r = jnp.where(lane < 64, _seg_rsqrt(sq, lane, 0, 64), _seg_rsqrt(sq, lane, 64, 96))
        qh = (blk * qa + pltpu.roll(blk, 96, 1) * qb) * r
        if sample:
            qabs_ref[:, hd * LANES:(hd + 1) * LANES] = _dot(qh, wabs_ref[hd])
            qrope_ref[:, hd * LANES:(hd + 1) * LANES] = jnp.where(
                lane < MLA_ROPE, pltpu.roll(qh, 64, 1), 0.0)
        else:
            qhat_ref[:, hd * LANES:(hd + 1) * LANES] = qh

    lat = _rms(z_c) * kvg_ref[...]
    rk = _seg_rsqrt(z_r * z_r, lane, 64, 96)
    kr = (z_r * ka_ref[...] + pltpu.roll(z_r, 96, 1) * kb_ref[...]) * rk
    rows_ref[:, 0:KV_LORA] = lat
    rows_ref[:, KV_LORA:MLA_ROW] = pltpu.roll(kr, 64, 1)[:, 0:MLA_ROPE]

    if not sample:
        kx = _dot(lat, wuk_ref[...])
        kng = kng_ref[...]
        for hd in range(MLA_HEADS):
            blk = kx[:, hd * LANES:(hd + 1) * LANES]
            r = _seg_rsqrt(blk * blk, lane, 0, 64)
            khat_ref[:, hd * LANES:(hd + 1) * LANES] = blk * r * kng + kr
        v_ref[...] = _dot(lat, wuv_ref[...])


def _dif_proj_kernel(x_ref, g_ref, win_ref, qg_ref, kg_ref, cqg_ref,
                     qn_ref, kn_ref, v_ref, cqn_ref, gs_ref):
    tm = x_ref.shape[0]
    lane = _iota((tm, LANES), 1)
    h = _rms(x_ref[...]) * g_ref[...]
    z = _dot(h, win_ref[...])
    o = DIF_QW + 2 * DIF_KW
    _cross_query_and_gate(z[:, o:o + MEM_W], z[:, o + MEM_W:], cqg_ref[...], cqn_ref, gs_ref, lane)
    qg = qg_ref[...]
    for hd in range(DIF_HEADS):
        qn_ref[:, hd * LANES:(hd + 1) * LANES] = _norm_halves(z[:, hd * LANES:(hd + 1) * LANES], lane) * qg
    kg = kg_ref[...]
    for hd in range(DIF_KV_HEADS):
        blk = z[:, DIF_QW + hd * LANES:DIF_QW + (hd + 1) * LANES]
        kn_ref[0, hd] = _norm_halves(blk, lane) * kg
        v_ref[0, hd] = z[:, DIF_QW + DIF_KW + hd * LANES:DIF_QW + DIF_KW + (hd + 1) * LANES]


def _full(shape):
    nd = len(shape)
    return pl.BlockSpec(shape, lambda i: (0,) * nd)


def _mla_proj(x, pos_tabs, lw, sample):
    n = x.shape[0]
    tm = min(PROJ_TM, n)
    npos = pos_tabs[0].shape[0] // tm
    tok = lambda w: pl.BlockSpec((tm, w), lambda i: (i, 0))
    pos = pl.BlockSpec((tm, LANES), lambda i: (i % npos, 0))
    ins = [x, lw["g"], lw["w_in"], lw["q_a_g"], lw["w_q"], lw["kv_a_g"], *pos_tabs, lw["cq_g"]]
    specs = [tok(D_MODEL), _full(lw["g"].shape), _full(lw["w_in"].shape), _full(lw["q_a_g"].shape),
             _full(lw["w_q"].shape), _full(lw["kv_a_g"].shape), pos, pos, pos, pos,
             _full(lw["cq_g"].shape)]
    hw = MLA_HEADS * LANES
    if sample:
        ins += [lw["w_abs"]]
        specs += [_full(lw["w_abs"].shape)]
        outs = [(hw, tok(hw)), (hw, tok(hw))]
    else:
        ins += [lw["w_uk_pad"], lw["kn_g"], lw["w_uv"]]
        specs += [_full(lw["w_uk_pad"].shape), _full(lw["kn_g"].shape), _full(lw["w_uv"].shape)]
        outs = [(hw, tok(hw)), (hw, tok(hw)), (SELF_W, tok(SELF_W))]
    outs += [(MLA_ROW, tok(MLA_ROW)), (MEM_W, tok(MEM_W)), (MIX_W, tok(MIX_W))]
    return pl.pallas_call(
        functools.partial(_mla_proj_kernel, sample),
        grid=(n // tm,),
        in_specs=specs,
        out_specs=[s for _, s in outs],
        out_shape=[jax.ShapeDtypeStruct((n, w), F32) for w, _ in outs],
        compiler_params=_params(("parallel",)),
        name="mla_proj_sample" if sample else "mla_proj_prompt",
    )(*ins)


def _dif_proj(x, lw, nb):
    n = x.shape[0]
    tpb = n // nb
    tm = min(PROJ_TM, tpb)
    per = tpb // tm
    tok = lambda w: pl.BlockSpec((tm, w), lambda i: (i, 0))
    kv = pl.BlockSpec((1, DIF_KV_HEADS, tm, DIF_VD), lambda i: (i // per, 0, i % per, 0))
    kv_shape = jax.ShapeDtypeStruct((nb, DIF_KV_HEADS, tpb, DIF_VD), F32)
    ins = [x, lw["g"], lw["w_in"], lw["q_g"], lw["k_g"], lw["cq_g"]]
    specs = [tok(D_MODEL)] + [_full(a.shape) for a in ins[1:]]
    flat = lambda w: jax.ShapeDtypeStruct((n, w), F32)
    return pl.pallas_call(
        _dif_proj_kernel,
        grid=(n // tm,),
        in_specs=specs,
        out_specs=[tok(DIF_QW), kv, kv, tok(MEM_W), tok(MIX_W)],
        out_shape=[flat(DIF_QW), kv_shape, kv_shape, flat(MEM_W), flat(MIX_W)],
        compiler_params=_params(("parallel",)),
        name="dif_proj",
    )(*ins)


def _mem_kv_kernel(mem_ref, g_ref, w_ref, ckg_ref, mk_ref, mv_ref):
    m = mem_ref.shape[1]
    lane = _iota((m, LANES), 1)
    kv = _dot(_rms(mem_ref[0]) * g_ref[0], w_ref[0])
    ckg = ckg_ref[0]
    for p in range(MEM_W // LANES):
        mk_ref[0, 0, :, p * LANES:(p + 1) * LANES] = _norm_halves(kv[:, p * LANES:(p + 1) * LANES], lane) * ckg
    mv_ref[0, 0] = kv[:, MEM_W:]


def _mem_kv(mem, g, w, ckg):
    b, m, _ = mem.shape
    depth = g.shape[0]
    out = pl.BlockSpec((1, 1, m, MEM_W), lambda l, i: (i, l, 0, 0))
    return pl.pallas_call(
        _mem_kv_kernel,
        grid=(depth, b),
        in_specs=[pl.BlockSpec((1, m, D_MODEL), lambda l, i: (i, 0, 0)),
                  pl.BlockSpec((1, 1, D_MODEL), lambda l, i: (l, 0, 0)),
                  pl.BlockSpec((1, D_MODEL, 2 * MEM_W), lambda l, i: (l, 0, 0)),
                  pl.BlockSpec((1, 1, LANES), lambda l, i: (l, 0, 0))],
        out_specs=[out, out],
        out_shape=[jax.ShapeDtypeStruct((b, depth, m, MEM_W), F32)] * 2,
        compiler_params=_params(("parallel", "parallel")),
        name="mem_kv",
    )(mem, g, w, ckg)


def _online_step(s, v, carry):
    m, l, acc = carry
    m_new = jnp.maximum(m, jnp.max(s, axis=-1, keepdims=True))
    alpha = jnp.exp(m - m_new)
    p = jnp.exp(s - m_new)
    l = alpha * l + jnp.sum(p, axis=-1, keepdims=True)
    acc = alpha * acc + _dot(p, v)
    return m_new, l, acc


def _flash_init(t):
    return (jnp.full((t, 1), NEG_INF, F32), jnp.zeros((t, 1), F32), jnp.zeros((t, LANES), F32))


def _mla_flash_kernel(q_ref, k_ref, v_ref, o_ref):
    t = q_ref.shape[0]
    qi = pl.program_id(2)
    causal = _iota((t, t), 1) <= _iota((t, t), 0)
    lane = _iota((t, LANES), 1)
    outs = []
    for hh in range(2):
        q = q_ref[:, hh * LANES:(hh + 1) * LANES].astype(BF16)

        def step(j, carry, diag, q=q, hh=hh):
            rows = pl.ds(pl.multiple_of(j * t, t), t)
            s = _dot_nt(q, k_ref[rows, hh * LANES:(hh + 1) * LANES])
            if diag:
                s = jnp.where(causal, s, NEG_INF)
            return _online_step(s, v_ref[rows, :], carry)

        carry = lax.fori_loop(0, qi, functools.partial(step, diag=False), _flash_init(t))
        _, l, acc = step(qi, carry, True)
        outs.append(acc / l)
    o_ref[...] = jnp.where(lane < MLA_VD, outs[0], outs[1])


def _mla_flash(qhat, khat, v, b, t):
    tq = min(FLASH_T, t)
    nq = t // tq
    return pl.pallas_call(
        _mla_flash_kernel,
        grid=(b, MLA_HEADS // 2, nq),
        in_specs=[pl.BlockSpec((tq, 2 * LANES), lambda i, p, j: (i * nq + j, p)),
                  pl.BlockSpec((t, 2 * LANES), lambda i, p, j: (i, p)),
                  pl.BlockSpec((t, LANES), lambda i, p, j: (i, p))],
        out_specs=pl.BlockSpec((tq, LANES), lambda i, p, j: (i * nq + j, p)),
        out_shape=jax.ShapeDtypeStruct((b * t, SELF_W), F32),
        compiler_params=_params(("parallel", "parallel", "arbitrary")),
        name="mla_flash",
    )(qhat, khat, v)


def _dif_lambda(lp, lam_init):
    a = jnp.sum(lp[0:1] * lp[1:2], axis=-1, keepdims=True)
    b = jnp.sum(lp[2:3] * lp[3:4], axis=-1, keepdims=True)
    return jnp.exp(a) - jnp.exp(b) + lam_init


def _dif_flash_kernel(lam_init, q_ref, k_ref, v_ref, slope_ref, lam_ref, sg_ref, o_ref):
    t = q_ref.shape[0]
    qi = pl.program_id(2)
    row = _iota((t, t), 0)
    col = _iota((t, t), 1)
    causal = col <= row
    dist0 = (row - col).astype(F32)
    lane = _iota((t, LANES), 1)
    lam = _dif_lambda(lam_ref[...], lam_init)
    for g in range(2):
        qblk = q_ref[:, g * LANES:(g + 1) * LANES]
        nsl = -slope_ref[0, g:g + 1, 0:1]
        nb0 = nsl * dist0
        os = []
        for c in range(2):
            qc = jnp.where((lane >= c * DIF_HD) & (lane < (c + 1) * DIF_HD), qblk, 0.0).astype(BF16)

            def step(j, carry, diag, qc=qc):
                rows = pl.ds(pl.multiple_of(j * t, t), t)
                off = jnp.full((1, 1), (qi - j) * t, jnp.int32).astype(F32)
                s = _dot_nt(qc, k_ref[0, 0, rows, :]) + (nb0 + nsl * off)
                if diag:
                    s = jnp.where(causal, s, NEG_INF)
                return _online_step(s, v_ref[0, 0, rows, :], carry)

            carry = lax.fori_loop(0, qi, functools.partial(step, diag=False), _flash_init(t))
            _, l, acc = step(qi, carry, True)
            os.append(acc / l)
        a = os[0] - lam * os[1]
        o_ref[:, g * LANES:(g + 1) * LANES] = _rms(a) * sg_ref[...] * (1.0 - lam_init)


def _dif_flash(qn, kn, v, slopes, lam_p, sub_g, lam_init, b, t):
    tq = min(FLASH_T, t)
    nq = t // tq
    return pl.pallas_call(
        functools.partial(_dif_flash_kernel, lam_init),
        grid=(b, DIF_KV_HEADS, nq),
        in_specs=[pl.BlockSpec((tq, 2 * LANES), lambda i, p, j: (i * nq + j, p)),
                  pl.BlockSpec((1, 1, t, DIF_VD), lambda i, p, j: (i, p, 0, 0)),
                  pl.BlockSpec((1, 1, t, DIF_VD), lambda i, p, j: (i, p, 0, 0)),
                  pl.BlockSpec((1, SUBLANES, LANES), lambda i, p, j: (p, 0, 0)),
                  pl.BlockSpec(lam_p.shape, lambda i, p, j: (0, 0)),
                  pl.BlockSpec(sub_g.shape, lambda i, p, j: (0, 0))],
        out_specs=pl.BlockSpec((tq, 2 * LANES), lambda i, p, j: (i * nq + j, p)),
        out_shape=jax.ShapeDtypeStruct((b * t, SELF_W), F32),
        compiler_params=_params(("parallel", "parallel", "arbitrary")),
        name="dif_flash",
    )(qn, kn, v, slopes, lam_p, sub_g)


def _sample_cross(cq8, mkt, mvt):
    row = _iota((SUBLANES, MEM_W), 0)
    head = jnp.right_shift(_iota((SUBLANES, MEM_W), 1), 6)
    upper = (row >= 4).astype(jnp.int32)
    cqd = jnp.where(row < 4, cq8, pltpu.roll(cq8, 4, 0))
    q16 = jnp.concatenate([jnp.where(head == upper, cqd, 0.0),
                           jnp.where(head == 2 + upper, cqd, 0.0)], axis=0)
    s = _dot(q16, mkt)
    p = jnp.exp(s - jnp.max(s, axis=-1, keepdims=True))
    o = _dot_nt(p, mvt) / jnp.sum(p, axis=-1, keepdims=True)
    o0 = o[0:SUBLANES]
    o1 = o[SUBLANES:2 * SUBLANES]
    t0 = jnp.where(head == 0, o0, pltpu.roll(o0, 4, 0))
    t1 = jnp.where(head == 2, o1, pltpu.roll(o1, 4, 0))
    return jnp.where(head < 2, t0, t1)


def _softmax_update(s, m_sc, l_sc, acc_sc, pv):
    m_prev = m_sc[...]
    m_new = jnp.maximum(m_prev, jnp.max(s, axis=-1, keepdims=True))
    alpha = jnp.exp(m_prev - m_new)
    p = jnp.exp(s - m_new[:, 0:1])
    l_sc[...] = alpha * l_sc[...] + jnp.sum(p, axis=-1, keepdims=True)
    acc_sc[...] = alpha * acc_sc[...] + pv(p)
    m_sc[...] = m_new


def _mla_sample_kernel(pps, pt_ref, *refs):
    pages = refs[:pps]
    lhsq_ref, wt_ref, new_ref, wuv_ref, cq_ref, mk_ref, mv_ref = refs[pps:pps + 7]
    self_ref, cross_ref = refs[pps + 7:pps + 9]
    lhs_sc, m_sc, l_sc, acc_sc, tile_sc, new_sc = refs[pps + 9:]
    del pt_ref
    c = pl.program_id(1)
    nrow = MLA_HEADS * 4
    nk = MLA_HEADS * MLA_NOPE

    @pl.when(c == 0)
    def _():
        lhs_sc[0:nk, :] = wt_ref[...]
        lhs_sc[nk:nk + 2 * nrow, :] = lhsq_ref[0].astype(BF16)
        m_sc[...] = jnp.full(m_sc.shape, NEG_INF, F32)
        l_sc[...] = jnp.zeros(l_sc.shape, F32)
        acc_sc[...] = jnp.zeros(acc_sc.shape, F32)
        tile_sc[...] = jnp.zeros(tile_sc.shape, BF16)
        new_sc[...] = jnp.zeros(new_sc.shape, BF16)
        new_sc[0:MLA_ROW, :] = new_ref[0].astype(BF16)
        cross_ref[0] = _sample_cross(cq_ref[0], mk_ref[0, 0], mv_ref[0, 0])

    def process(tb, is_new):
        w = tb.shape[1]
        r_all = _dot(lhs_sc[...], tb)
        rs = []
        for hd in range(MLA_HEADS):
            kh = r_all[hd * MLA_NOPE:(hd + 1) * MLA_NOPE, :]
            ss = jnp.sum(kh * kh, axis=0, keepdims=True)
            rs.append(lax.rsqrt(ss * (1.0 / MLA_NOPE) + EPS))
        row8 = _iota((SUBLANES, w), 0)
        r48 = jnp.concatenate([jnp.where(row8 < 4, rs[2 * i], rs[2 * i + 1])
                               for i in range(MLA_HEADS // 2)], axis=0)
        s = r_all[nk:nk + nrow, :] * r48 + r_all[nk + nrow:nk + 2 * nrow, :]
        if is_new:
            qpos = _iota((nrow, w), 0) & 3
            s = jnp.where(_iota((nrow, w), 1) <= qpos, s, NEG_INF)
        _softmax_update(s, m_sc, l_sc, acc_sc, lambda p: _dot_nt(p, tb[0:KV_LORA, :]))

    for i in range(pps // 2):
        tile_sc[0:MLA_ROW, 0:PAGE] = pages[2 * i][0, 0].astype(BF16)
        tile_sc[0:MLA_ROW, PAGE:2 * PAGE] = pages[2 * i + 1][0, 0].astype(BF16)
        process(tile_sc[...], False)

    @pl.when(c == pl.num_programs(1) - 1)
    def _():
        process(new_sc[...], True)
        o_lat = acc_sc[...] / l_sc[...]
        lane = _iota((SUBLANES, LANES), 1)
        for i in range(MLA_HEADS // 2):
            res = _dot(o_lat[i * SUBLANES:(i + 1) * SUBLANES], wuv_ref[:, i * LANES:(i + 1) * LANES])
            self_ref[0, :, i * LANES:(i + 1) * LANES] = jnp.where(lane < MLA_VD, res, pltpu.roll(res, 4, 0))


def _page_spec(page_shape, layer_j, n_pages, pps, i):
    def index(s, c, pt):
        return (pt[s * n_pages + c * pps + i], layer_j) + (0,) * len(page_shape)
    return pl.BlockSpec((1, 1) + page_shape, index)


def _mla_sample(page_table, cache, layer_j, lhsq, w_t, rows_new, w_uv, cq8, memk, memv, layer_i):
    db, n_pages = page_table.shape
    pps = min(PAGES_PER_STEP, n_pages)
    nc = n_pages // pps
    nrow = MLA_HEADS * 4
    seq = lambda shape: pl.BlockSpec((1,) + shape, lambda s, c, pt: (s,) + (0,) * len(shape))
    const = lambda shape: pl.BlockSpec(shape, lambda s, c, pt: (0,) * len(shape))
    mem = pl.BlockSpec((1, 1, MEM_LEN, MEM_W), lambda s, c, pt: (s, layer_i, 0, 0))
    grid_spec = pltpu.PrefetchScalarGridSpec(
        num_scalar_prefetch=1,
        grid=(db, nc),
        in_specs=[_page_spec((MLA_ROW, PAGE), layer_j, n_pages, pps, i) for i in range(pps)] + [
            seq((2 * nrow, ROW_PAD)), const(w_t.shape), seq((MLA_ROW, LANES)), const(w_uv.shape),
            seq((SUBLANES, MEM_W)), mem, mem],
        out_specs=[seq((SUBLANES, SELF_W)), seq((SUBLANES, MEM_W))],
        scratch_shapes=[pltpu.VMEM((MLA_HEADS * MLA_NOPE + 2 * nrow, ROW_PAD), BF16),
                        pltpu.VMEM((nrow, LANES), F32), pltpu.VMEM((nrow, LANES), F32),
                        pltpu.VMEM((nrow, LANES), F32), pltpu.VMEM((ROW_PAD, 2 * PAGE), BF16),
                        pltpu.VMEM((ROW_PAD, LANES), BF16)])
    return pl.pallas_call(
        functools.partial(_mla_sample_kernel, pps),
        grid_spec=grid_spec,
        out_shape=[jax.ShapeDtypeStruct((db, SUBLANES, SELF_W), F32),
                   jax.ShapeDtypeStruct((db, SUBLANES, MEM_W), F32)],
        compiler_params=_params(("arbitrary", "arbitrary")),
        name="mla_sample",
    )(page_table.reshape(-1), *([cache] * pps), lhsq, w_t, rows_new, w_uv, cq8, memk, memv)


def _dif_sample_kernel(pps, lam_init, past, pt_ref, *refs):
    kpages = refs[:pps]
    vpages = refs[pps:2 * pps]
    (lhsq_ref, knew_ref, vnew_ref, nsl_ref, lam_ref, sg_ref,
     cq_ref, mk_ref, mv_ref) = refs[2 * pps:2 * pps + 9]
    self_ref, cross_ref = refs[2 * pps + 9:2 * pps + 11]
    m_sc, l_sc, acc_sc, knew_sc, vnew_sc = refs[2 * pps + 11:]
    del pt_ref
    c = pl.program_id(1)
    nrow = DIF_KV_HEADS * 16

    @pl.when(c == 0)
    def _():
        m_sc[...] = jnp.full(m_sc.shape, NEG_INF, F32)
        l_sc[...] = jnp.zeros(l_sc.shape, F32)
        acc_sc[...] = jnp.zeros(acc_sc.shape, F32)
        knew_sc[...] = jnp.zeros(knew_sc.shape, F32)
        vnew_sc[...] = jnp.zeros(vnew_sc.shape, F32)
        knew_sc[0:SUBLANES, :] = knew_ref[0]
        vnew_sc[0:SUBLANES, :] = vnew_ref[0]
        cross_ref[0] = _sample_cross(cq_ref[0], mk_ref[0, 0], mv_ref[0, 0])

    def process(kts, vts, base, is_new):
        w = kts[0].shape[0]
        q = lhsq_ref[0]
        s = jnp.concatenate(
            [_dot_nt(q[kv * 16:(kv + 1) * 16], kts[kv]) for kv in range(DIF_KV_HEADS)], axis=0)
        qrow = _iota((nrow, w), 0) & 3
        col = _iota((nrow, w), 1)
        dist = (past + qrow - base - col).astype(F32)
        s = s + nsl_ref[:, 0:1] * dist
        if is_new:
            s = jnp.where(col <= qrow, s, NEG_INF)

        def pv(p):
            return jnp.concatenate(
                [_dot(p[kv * 16:(kv + 1) * 16], vts[kv]) for kv in range(DIF_KV_HEADS)], axis=0)
        _softmax_update(s, m_sc, l_sc, acc_sc, pv)

    def pair(pg, i):
        return [jnp.concatenate([pg[2 * i][0, 0, kv], pg[2 * i + 1][0, 0, kv]], axis=0)
                for kv in range(DIF_KV_HEADS)]

    for i in range(pps // 2):
        process(pair(kpages, i), pair(vpages, i), (c * pps + 2 * i) * PAGE, False)

    @pl.when(c == pl.num_programs(1) - 1)
    def _():
        heads = lambda ref: [ref[:, kv * LANES:(kv + 1) * LANES] for kv in range(DIF_KV_HEADS)]
        process(heads(knew_sc), heads(vnew_sc), past, True)
        a = acc_sc[...] / l_sc[...]
        lam = _dif_lambda(lam_ref[...], lam_init)
        for i in range(DIF_HEADS):
            an = a[i * SUBLANES:(i + 1) * SUBLANES]
            o = an - lam * pltpu.roll(an, 4, 0)
            self_ref[0, :, i * LANES:(i + 1) * LANES] = _rms(o) * sg_ref[...] * (1.0 - lam_init)


def _dif_sample(page_table, cache_k, cache_v, layer_j, lhsq, k_new, v_new, nsl, lam_p, sub_g,
                lam_init, cq8, memk, memv, layer_i):
    db, n_pages = page_table.shape
    pps = min(PAGES_PER_STEP, n_pages)
    nc = n_pages // pps
    nrow = DIF_KV_HEADS * 16
    seq = lambda shape: pl.BlockSpec((1,) + shape, lambda s, c, pt: (s,) + (0,) * len(shape))
    const = lambda shape: pl.BlockSpec(shape, lambda s, c, pt: (0,) * len(shape))
    mem = pl.BlockSpec((1, 1, MEM_LEN, MEM_W), lambda s, c, pt: (s, layer_i, 0, 0))
    pages = [_page_spec((DIF_KV_HEADS, PAGE, DIF_VD), layer_j, n_pages, pps, i) for i in range(pps)]
    grid_spec = pltpu.PrefetchScalarGridSpec(
        num_scalar_prefetch=1,
        grid=(db, nc),
        in_specs=pages + pages + [
            seq((nrow, LANES)), seq((SUBLANES, DIF_KW)), seq((SUBLANES, DIF_KW)), const(nsl.shape),
            const(lam_p.shape), const(sub_g.shape), seq((SUBLANES, MEM_W)), mem, mem],
        out_specs=[seq((SUBLANES, SELF_W)), seq((SUBLANES, MEM_W))],
        scratch_shapes=[pltpu.VMEM((nrow, LANES), F32), pltpu.VMEM((nrow, LANES), F32),
                        pltpu.VMEM((nrow, LANES), F32), pltpu.VMEM((PAGE, DIF_KW), F32),
                        pltpu.VMEM((PAGE, DIF_KW), F32)])
    return pl.pallas_call(
        functools.partial(_dif_sample_kernel, pps, lam_init, n_pages * PAGE),
        grid_spec=grid_spec,
        out_shape=[jax.ShapeDtypeStruct((db, SUBLANES, SELF_W), F32),
                   jax.ShapeDtypeStruct((db, SUBLANES, MEM_W), F32)],
        compiler_params=_params(("arbitrary", "arbitrary")),
        name="dif_sample",
    )(page_table.reshape(-1), *([cache_k] * pps), *([cache_v] * pps), lhsq, k_new, v_new, nsl,
      lam_p, sub_g, cq8, memk, memv)


def _out_kernel(with_cross, x_ref, self_ref, c_ref, gs_ref, *rest):
    if with_cross:
        mk_ref, mv_ref, w_ref, y_ref = rest
        tm = x_ref.shape[0]
        cq = c_ref[...]
        mk = mk_ref[0, 0].astype(BF16)
        mv = mv_ref[0, 0].astype(BF16)
        head = jnp.right_shift(_iota((tm, MEM_W), 1), 6)
        cross = jnp.zeros((tm, MEM_W), F32)
        for hd in range(MEM_HEADS):
            s = _dot_nt(jnp.where(head == hd, cq, 0.0), mk)
            p = jnp.exp(s - jnp.max(s, axis=-1, keepdims=True))
            o = _dot(p, mv) / jnp.sum(p, axis=-1, keepdims=True)
            cross = jnp.where(head == hd, o, cross)
    else:
        w_ref, y_ref = rest
        cross = c_ref[...]
    mix = jnp.concatenate([self_ref[...], cross], axis=-1) * gs_ref[...]
    y_ref[...] = x_ref[...] + _dot(mix, w_ref[...])


def _out_proj(x, self_o, c, gs, w, memk=None, memv=None, layer_i=0, tokens_per_batch=None):
    n = x.shape[0]
    tm = min(PROJ_TM, n)
    tok = lambda wd: pl.BlockSpec((tm, wd), lambda i: (i, 0))
    ins = [x, self_o, c, gs]
    specs = [tok(D_MODEL), tok(SELF_W), tok(MEM_W), tok(MIX_W)]
    with_cross = memk is not None
    if with_cross:
        per = tokens_per_batch // tm
        mem = pl.BlockSpec((1, 1, MEM_LEN, MEM_W), lambda i: (i // per, layer_i, 0, 0))
        ins += [memk, memv]
        specs += [mem, mem]
    ins.append(w)
    specs.append(_full(w.shape))
    return pl.pallas_call(
        functools.partial(_out_kernel, with_cross),
        grid=(n // tm,),
        in_specs=specs,
        out_specs=tok(D_MODEL),
        out_shape=jax.ShapeDtypeStruct((n, D_MODEL), F32),
        compiler_params=_params(("parallel",)),
        name="out_proj_prompt" if with_cross else "out_proj_sample",
    )(*ins)


def _rope_tables(pos, g_nope, g_rope, scale):
    half = MLA_ROPE // 2
    inv_freq = ROPE_THETA ** (-jnp.arange(half, dtype=F32) / half)
    ang = pos.astype(F32)[:, None] * inv_freq
    c, s = jnp.cos(ang), jnp.sin(ang)
    g1, g2 = g_rope[:half].astype(F32), g_rope[half:].astype(F32)
    n = pos.shape[0]
    z32 = jnp.zeros((n, LANES - MLA_NOPE - MLA_ROPE), F32)
    a = jnp.concatenate([jnp.broadcast_to(g_nope.astype(F32), (n, MLA_NOPE)), g1 * c, g2 * c, z32], axis=1)
    b = jnp.concatenate([jnp.zeros((n, MLA_NOPE), F32), -g2 * s, g1 * s, z32], axis=1)
    return a * scale, b * scale


def _row(v):
    return v.astype(F32).reshape(1, -1)


def _mla_weights(i, j, norm_g, cq_norm_g, mla_w_in, mla_q_a_norm_g, mla_w_q_up, mla_kv_a_norm_g,
                 mla_w_uk, mla_w_uv, mla_kn_g):
    w = mla_w_in[j]
    c0 = Q_LORA + KV_LORA
    x1 = w[:, c0:c0 + 16]
    x2 = w[:, c0 + 16:c0 + 32]
    w_in = jnp.concatenate([w[:, :c0], jnp.zeros((D_MODEL, 64), w.dtype), x1, x2, x2, x1,
                            w[:, c0 + 32:]], axis=1).astype(BF16)
    wq = mla_w_q_up[j].reshape(Q_LORA, MLA_HEADS, MLA_NOPE + MLA_ROPE)
    qn, q1, q2 = wq[..., :MLA_NOPE], wq[..., MLA_NOPE:MLA_NOPE + 16], wq[..., MLA_NOPE + 16:]
    w_q = jnp.concatenate([qn, q1, q2, q2, q1], axis=-1).reshape(Q_LORA, MLA_HEADS * LANES).astype(BF16)
    wuk = mla_w_uk[j].reshape(KV_LORA, MLA_HEADS, MLA_NOPE)
    w_uk_pad = jnp.pad(wuk, ((0, 0), (0, 0), (0, LANES - MLA_NOPE))).reshape(KV_LORA, -1).astype(BF16)
    w_t = jnp.pad(mla_w_uk[j].T, ((0, 0), (0, ROW_PAD - KV_LORA))).astype(BF16)
    wabs = jnp.transpose(wuk * mla_kn_g[j].astype(F32), (1, 2, 0))
    w_abs = jnp.pad(wabs, ((0, 0), (0, LANES - MLA_NOPE), (0, 0))).astype(BF16)
    kn_g = _row(jnp.pad(mla_kn_g[j], (0, LANES - MLA_NOPE)))
    return dict(g=_row(norm_g[i]), w_in=w_in, q_a_g=_row(mla_q_a_norm_g[j]), w_q=w_q,
                kv_a_g=_row(mla_kv_a_norm_g[j]), w_uk_pad=w_uk_pad, w_t=w_t, w_abs=w_abs, kn_g=kn_g,
                w_uv=mla_w_uv[j].astype(BF16),
                cq_g=_row(jnp.tile(cq_norm_g[i], 2)) * MEM_HD ** -0.5)


def kernel(x_prompt, x_sample, cache_mla_kv, cache_diff_k, cache_diff_v, cache_mem_k, cache_mem_v,
           page_table, mem_prompt, norm_g, w_out, mem_norm_g, w_mem_kv, cq_norm_g, ck_norm_g,
           mla_w_in, mla_q_a_norm_g, mla_w_q_up, mla_kv_a_norm_g, mla_w_uk, mla_w_uv,
           mla_qn_g, mla_kn_g, mla_qr_g, mla_kr_g,
           dif_w_in, dif_q_g, dif_k_g, dif_lambda, dif_subln_g):
    b, t, d = x_prompt.shape
    db, ds, _ = x_sample.shape
    n_pages = page_table.shape[1]
    past = n_pages * PAGE
    depth = norm_g.shape[0]
    assert d == D_MODEL and ds == 4 and t % min(PROJ_TM, FLASH_T) == 0
    assert n_pages % min(PAGES_PER_STEP, n_pages) == 0 and min(PAGES_PER_STEP, n_pages) % 2 == 0

    pos_p = jnp.arange(t, dtype=jnp.int32)
    pos_s = jnp.tile(past + jnp.arange(ds, dtype=jnp.int32), db)
    xp = x_prompt.reshape(b * t, d)
    xs = x_sample.reshape(db * ds, d)
    w_out_b = w_out.astype(BF16)

    memk, memv = _mem_kv(mem_prompt, mem_norm_g.astype(F32).reshape(depth, 1, d), w_mem_kv.astype(BF16),
                         jnp.tile(ck_norm_g.astype(F32), (1, 2)).reshape(depth, 1, LANES))
    smem_k = jnp.transpose(cache_mem_k, (0, 1, 3, 4, 2)).reshape(db, depth, MEM_W, MEM_LEN)
    smem_v = jnp.transpose(cache_mem_v, (0, 1, 3, 4, 2)).reshape(db, depth, MEM_W, MEM_LEN)
    cache_mla_t = jnp.swapaxes(cache_mla_kv, 2, 3)
    cache_dk = jnp.transpose(cache_diff_k, (0, 1, 3, 2, 4))
    cache_dv = jnp.transpose(cache_diff_v, (0, 1, 3, 2, 4))

    def pad8(a, k=1):
        return jnp.pad(a.reshape(db, ds, -1), ((0, 0), (0, k * SUBLANES - ds), (0, 0)))

    mla_rows_p, mla_rows_s, dk_p, dv_p, dk_s, dv_s = [], [], [], [], [], []
    slopes = 2.0 ** (-8.0 * jnp.arange(1, DIF_HEADS + 1, dtype=F32) / DIF_HEADS)

    for i in range(depth):
        j = i // 2
        if i % 2 == 0:
            lw = _mla_weights(i, j, norm_g, cq_norm_g, mla_w_in, mla_q_a_norm_g, mla_w_q_up,
                              mla_kv_a_norm_g, mla_w_uk, mla_w_uv, mla_kn_g)
            scale = (MLA_NOPE + MLA_ROPE) ** -0.5
            tabs_p = (*_rope_tables(pos_p, mla_qn_g[j], mla_qr_g[j], scale),
                      *_rope_tables(pos_p, jnp.zeros((MLA_NOPE,), F32), mla_kr_g[j], 1.0))
            tabs_s = (*_rope_tables(pos_s, mla_qn_g[j], mla_qr_g[j], scale),
                      *_rope_tables(pos_s, jnp.zeros((MLA_NOPE,), F32), mla_kr_g[j], 1.0))
            qhat, khat, v_p, rows_p, cq_p, gs_p = _mla_proj(xp, tabs_p, lw, False)
            qabs, qrope, rows_s, cq_s, gs_s = _mla_proj(xs, tabs_s, lw, True)
            self_p = _mla_flash(qhat, khat, v_p, b, t)

            def to_rows(a):
                return a.reshape(db, ds, MLA_HEADS, LANES).transpose(0, 2, 1, 3).reshape(db, MLA_HEADS * ds, LANES)
            lhs_n = jnp.pad(to_rows(qabs), ((0, 0), (0, 0), (0, ROW_PAD - KV_LORA)))
            lhs_r = jnp.pad(to_rows(qrope), ((0, 0), (0, 0), (KV_LORA, 0)))
            lhsq = jnp.concatenate([lhs_n, lhs_r], axis=1)
            new_t = jnp.pad(jnp.swapaxes(rows_s.reshape(db, ds, MLA_ROW), 1, 2),
                            ((0, 0), (0, 0), (0, LANES - ds)))
            self_s, cross_s = _mla_sample(page_table, cache_mla_t, j, lhsq, lw["w_t"], new_t,
                                          lw["w_uv"], pad8(cq_s), smem_k, smem_v, i)
            mla_rows_p.append(rows_p.reshape(b, t, MLA_ROW))
            mla_rows_s.append(rows_s.reshape(db, ds, MLA_ROW))
        else:
            lam_init = 0.8 - 0.6 * math.exp(-0.3 * i)
            lw = dict(g=_row(norm_g[i]), w_in=dif_w_in[j].astype(BF16),
                      q_g=_row(jnp.tile(dif_q_g[j], 2)) * DIF_HD ** -0.5, k_g=_row(jnp.tile(dif_k_g[j], 2)),
                      cq_g=_row(jnp.tile(cq_norm_g[i], 2)) * MEM_HD ** -0.5)
            lam_p = dif_lambda[j].astype(F32)
            sub_g = _row(dif_subln_g[j])
            qn_p, kn_p, v_p, cq_p, gs_p = _dif_proj(xp, lw, b)
            qn_s, kn_s, v_s, cq_s, gs_s = _dif_proj(xs, lw, 1)
            to_tok = lambda a: jnp.transpose(a[0], (1, 0, 2)).reshape(db * ds, DIF_KW)
            kn_s, v_s = to_tok(kn_s), to_tok(v_s)
            slope_tab = jnp.broadcast_to(
                jnp.pad(slopes.reshape(DIF_KV_HEADS, 2), ((0, 0), (0, SUBLANES - 2)))[:, :, None],
                (DIF_KV_HEADS, SUBLANES, LANES))
            self_p = _dif_flash(qn_p, kn_p, v_p, slope_tab, lam_p, sub_g, lam_init, b, t)

            q5 = qn_s.reshape(db, ds, DIF_KV_HEADS, 2, 1, LANES).transpose(0, 2, 3, 4, 1, 5)
            lane_map = (jnp.arange(LANES) // DIF_HD)[None, None, None, None, None, :]
            q6 = jnp.where(lane_map == jnp.arange(2)[None, None, None, :, None, None], q5, 0.0)
            lhsq = q6.reshape(db, DIF_KV_HEADS * 16, LANES)
            nsl = jnp.broadcast_to(-jnp.repeat(slopes, 8)[:, None], (DIF_KV_HEADS * 16, LANES))
            self_s, cross_s = _dif_sample(page_table, cache_dk, cache_dv, j, lhsq, pad8(kn_s), pad8(v_s),
                                          nsl, lam_p, sub_g, lam_init, pad8(cq_s), smem_k, smem_v, i)
            dk_p.append(kn_p)
            dv_p.append(v_p)
            dk_s.append(kn_s.reshape(db, ds, DIF_KV_HEADS, DIF_VD))
            dv_s.append(v_s.reshape(db, ds, DIF_KV_HEADS, DIF_VD))

        xp = _out_proj(xp, self_p, cq_p, gs_p, w_out_b[i], memk, memv, i, t)
        xs = _out_proj(xs, self_s[:, :ds].reshape(db * ds, SELF_W), cross_s[:, :ds].reshape(db * ds, MEM_W),
                       gs_s, w_out_b[i])

    mem_shape = (b, depth, MEM_LEN, MEM_HEADS, MEM_HD)
    tok_major = lambda parts: jnp.transpose(jnp.stack(parts, axis=1), (0, 1, 3, 2, 4))
    return (xp.reshape(b, t, d), xs.reshape(db, ds, d),
            jnp.stack(mla_rows_p, axis=1), tok_major(dk_p), tok_major(dv_p),
            memk.reshape(mem_shape), memv.reshape(mem_shape),
            jnp.stack(mla_rows_s, axis=1), jnp.stack(dk_s, axis=1), jnp.stack(dv_s, axis=1))
```

```python
import functools
import math

import jax
import jax.numpy as jnp
from jax import lax
from jax.experimental import pallas as pl
from jax.experimental.pallas import tpu as pltpu

F32 = jnp.float32
BF16 = jnp.bfloat16

LANES = 128
SUBLANES = 8
VMEM_LIMIT = 56 * 1024 * 1024

D_MODEL = 1024
PAGE = 128
MEM_LEN = 256
MEM_HEADS = 4
MEM_HD = 64
MEM_W = MEM_HEADS * MEM_HD
MIX_W = D_MODEL
SELF_W = MIX_W - MEM_W
MLA_HEADS = 12
MLA_NOPE = 64
MLA_ROPE = 32
MLA_VD = 64
Q_LORA = 256
KV_LORA = 128
MLA_ROW = KV_LORA + MLA_ROPE
ROW_PAD = 2 * LANES
ROPE_THETA = 10000.0
DIF_HEADS = 6
DIF_KV_HEADS = 3
DIF_HD = 64
DIF_VD = 2 * DIF_HD
DIF_QW = DIF_HEADS * DIF_VD
DIF_KW = DIF_KV_HEADS * DIF_VD
EPS = 1e-6
NEG_INF = float("-inf")

PROJ_TM = 256
MLA_FLASH_T = 512
DIF_FLASH_T = 512
PAGES_PER_STEP = 16


def _dot(a, b):
    return jnp.dot(a.astype(BF16), b.astype(BF16), preferred_element_type=F32)


def _dot_nt(a, b):
    return lax.dot_general(a.astype(BF16), b.astype(BF16), (((1,), (1,)), ((), ())),
                           preferred_element_type=F32)


def _iota(shape, dim):
    return lax.broadcasted_iota(jnp.int32, shape, dim)


def _rms(x):
    return x * lax.rsqrt(jnp.mean(x * x, axis=-1, keepdims=True) + EPS)


def _seg_rsqrt(sq, lane, lo, hi):
    ss = jnp.sum(jnp.where((lane >= lo) & (lane < hi), sq, 0.0), axis=-1, keepdims=True)
    return lax.rsqrt(ss * (1.0 / (hi - lo)) + EPS)


def _norm_halves(blk, lane):
    sq = blk * blk
    r0 = _seg_rsqrt(sq, lane, 0, 64)
    r1 = _seg_rsqrt(sq, lane, 64, 128)
    return blk * jnp.where(lane < 64, r0, r1)


def _silu(x):
    return x / (1.0 + jnp.exp(-x))


def _params(sem):
    return pltpu.CompilerParams(dimension_semantics=sem, vmem_limit_bytes=VMEM_LIMIT)


def _cross_query_and_gate(z_cq, z_gate, cqg, cqn_ref, gs_ref, lane):
    for p in range(MEM_W // LANES):
        blk = z_cq[:, p * LANES:(p + 1) * LANES]
        cqn_ref[:, p * LANES:(p + 1) * LANES] = _norm_halves(blk, lane) * cqg
    gs_ref[...] = _silu(z_gate)


def _mla_proj_kernel(sample, x_ref, g_ref, win_ref, qag_ref, wq_ref, kvg_ref,
                     qa_ref, qb_ref, ka_ref, kb_ref, cqg_ref, *rest):
    if sample:
        wabs_ref, qabs_ref, qrope_ref, rows_ref, cqn_ref, gs_ref = rest
    else:
        wuk_ref, kng_ref, wuv_ref, qhat_ref, khat_ref, v_ref, rows_ref, cqn_ref, gs_ref = rest
    tm = x_ref.shape[0]
    lane = _iota((tm, LANES), 1)

    h = _rms(x_ref[...]) * g_ref[...]
    z = _dot(h, win_ref[...])
    z_q = z[:, 0:256]
    z_c = z[:, 256:384]
    z_r = z[:, 384:512]
    _cross_query_and_gate(z[:, 512:768], z[:, 768:1792], cqg_ref[...], cqn_ref, gs_ref, lane)

    q = _dot(_rms(z_q) * qag_ref[...], wq_ref[...])
    qa = qa_ref[...]
    qb = qb_ref[...]
    for hd in range(MLA_HEADS):
        blk = q[:, hd * LANES:(hd + 1) * LANES]
        sq = blk * blk
        r = jnp.where(lane < 64, _seg_rsqrt(sq, lane, 0, 64), _seg_rsqrt(sq, lane, 64, 96))
        qh = (blk * qa + pltpu.roll(blk, 96, 1) * qb) * r
        if sample:
            qabs_ref[:, hd * LANES:(hd + 1) * LANES] = _dot(qh, wabs_ref[hd])
            qrope_ref[:, hd * LANES:(hd + 1) * LANES] = jnp.where(
                lane < MLA_ROPE, pltpu.roll(qh, 64, 1), 0.0)
        else:
            qhat_ref[:, hd * LANES:(hd + 1) * LANES] = qh

    lat = _rms(z_c) * kvg_ref[...]
    rk = _seg_rsqrt(z_r * z_r, lane, 64, 96)
    kr = (z_r * ka_ref[...] + pltpu.roll(z_r, 96, 1) * kb_ref[...]) * rk
    rows_ref[:, 0:KV_LORA] = lat
    rows_ref[:, KV_LORA:MLA_ROW] = pltpu.roll(kr, 64, 1)[:, 0:MLA_ROPE]

    if not sample:
        kx = _dot(lat, wuk_ref[...])
        kng = kng_ref[...]
        for hd in range(MLA_HEADS):
            blk = kx[:, hd * LANES:(hd + 1) * LANES]
            r = _seg_rsqrt(blk * blk, lane, 0, 64)
            khat_ref[:, hd * LANES:(hd + 1) * LANES] = blk * r * kng + kr
        v_ref[...] = _dot(lat, wuv_ref[...])


def _dif_proj_kernel(x_ref, g_ref, win_ref, qg_ref, kg_ref, cqg_ref,
                     qn_ref, kn_ref, v_ref, cqn_ref, gs_ref):
    tm = x_ref.shape[0]
    lane = _iota((tm, LANES), 1)
    h = _rms(x_ref[...]) * g_ref[...]
    z = _dot(h, win_ref[...])
    o = DIF_QW + 2 * DIF_KW
    _cross_query_and_gate(z[:, o:o + MEM_W], z[:, o + MEM_W:], cqg_ref[...], cqn_ref, gs_ref, lane)
    qg = qg_ref[...]
    for hd in range(DIF_HEADS):
        qn_ref[:, hd * LANES:(hd + 1) * LANES] = _norm_halves(z[:, hd * LANES:(hd + 1) * LANES], lane) * qg
    kg = kg_ref[...]
    for hd in range(DIF_KV_HEADS):
        blk = z[:, DIF_QW + hd * LANES:DIF_QW + (hd + 1) * LANES]
        kn_ref[0, hd] = _norm_halves(blk, lane) * kg
        v_ref[0, hd] = z[:, DIF_QW + DIF_KW + hd * LANES:DIF_QW + DIF_KW + (hd + 1) * LANES]


def _full(shape):
    nd = len(shape)
    return pl.BlockSpec(shape, lambda i: (0,) * nd)


def _mla_proj(x, pos_tabs, lw, sample):
    n = x.shape[0]
    tm = min(PROJ_TM, n)
    npos = pos_tabs[0].shape[0] // tm
    tok = lambda w: pl.BlockSpec((tm, w), lambda i: (i, 0))
    pos = pl.BlockSpec((tm, LANES), lambda i: (i % npos, 0))
    ins = [x, lw["g"], lw["w_in"], lw["q_a_g"], lw["w_q"], lw["kv_a_g"], *pos_tabs, lw["cq_g"]]
    specs = [tok(D_MODEL), _full(lw["g"].shape), _full(lw["w_in"].shape), _full(lw["q_a_g"].shape),
             _full(lw["w_q"].shape), _full(lw["kv_a_g"].shape), pos, pos, pos, pos,
             _full(lw["cq_g"].shape)]
    hw = MLA_HEADS * LANES
    if sample:
        ins += [lw["w_abs"]]
        specs += [_full(lw["w_abs"].shape)]
        outs = [(hw, tok(hw)), (hw, tok(hw))]
    else:
        ins += [lw["w_uk_pad"], lw["kn_g"], lw["w_uv"]]
        specs += [_full(lw["w_uk_pad"].shape), _full(lw["kn_g"].shape), _full(lw["w_uv"].shape)]
        outs = [(hw, tok(hw)), (hw, tok(hw)), (SELF_W, tok(SELF_W))]
    outs += [(MLA_ROW, tok(MLA_ROW)), (MEM_W, tok(MEM_W)), (MIX_W, tok(MIX_W))]
    return pl.pallas_call(
        functools.partial(_mla_proj_kernel, sample),
        grid=(n // tm,),
        in_specs=specs,
        out_specs=[s for _, s in outs],
        out_shape=[jax.ShapeDtypeStruct((n, w), F32) for w, _ in outs],
        compiler_params=_params(("parallel",)),
        name="mla_proj_sample" if sample else "mla_proj_prompt",
    )(*ins)


def _dif_proj(x, lw, nb):
    n = x.shape[0]
    tpb = n // nb
    tm = min(PROJ_TM, tpb)
    per = tpb // tm
    tok = lambda w: pl.BlockSpec((tm, w), lambda i: (i, 0))
    kv = pl.BlockSpec((1, DIF_KV_HEADS, tm, DIF_VD), lambda i: (i // per, 0, i % per, 0))
    kv_shape = jax.ShapeDtypeStruct((nb, DIF_KV_HEADS, tpb, DIF_VD), F32)
    ins = [x, lw["g"], lw["w_in"], lw["q_g"], lw["k_g"], lw["cq_g"]]
    specs = [tok(D_MODEL)] + [_full(a.shape) for a in ins[1:]]
    flat = lambda w: jax.ShapeDtypeStruct((n, w), F32)
    return pl.pallas_call(
        _dif_proj_kernel,
        grid=(n // tm,),
        in_specs=specs,
        out_specs=[tok(DIF_QW), kv, kv, tok(MEM_W), tok(MIX_W)],
        out_shape=[flat(DIF_QW), kv_shape, kv_shape, flat(MEM_W), flat(MIX_W)],
        compiler_params=_params(("parallel",)),
        name="dif_proj",
    )(*ins)


def _mem_kv_kernel(mem_ref, g_ref, w_ref, ckg_ref, mk_ref, mv_ref):
    m = mem_ref.shape[1]
    lane = _iota((m, LANES), 1)
    kv = _dot(_rms(mem_ref[0]) * g_ref[0], w_ref[0])
    ckg = ckg_ref[0]
    for p in range(MEM_W // LANES):
        mk_ref[0, 0, :, p * LANES:(p + 1) * LANES] = _norm_halves(kv[:, p * LANES:(p + 1) * LANES], lane) * ckg
    mv_ref[0, 0] = kv[:, MEM_W:]


def _mem_kv(mem, g, w, ckg):
    b, m, _ = mem.shape
    depth = g.shape[0]
    out = pl.BlockSpec((1, 1, m, MEM_W), lambda l, i: (i, l, 0, 0))
    return pl.pallas_call(
        _mem_kv_kernel,
        grid=(depth, b),
        in_specs=[pl.BlockSpec((1, m, D_MODEL), lambda l, i: (i, 0, 0)),
                  pl.BlockSpec((1, 1, D_MODEL), lambda l, i: (l, 0, 0)),
                  pl.BlockSpec((1, D_MODEL, 2 * MEM_W), lambda l, i: (l, 0, 0)),
                  pl.BlockSpec((1, 1, LANES), lambda l, i: (l, 0, 0))],
        out_specs=[out, out],
        out_shape=[jax.ShapeDtypeStruct((b, depth, m, MEM_W), F32)] * 2,
        compiler_params=_params(("parallel", "parallel")),
        name="mem_kv",
    )(mem, g, w, ckg)


def _online_step(s, v, carry, shift=None):
    m, l, acc = carry
    smax = jnp.max(s, axis=-1, keepdims=True)
    if shift is not None:
        smax = smax + shift
    m_new = jnp.maximum(m, smax)
    alpha = jnp.exp(m - m_new)
    p = jnp.exp(s - (m_new if shift is None else m_new - shift))
    l = alpha * l + jnp.sum(p, axis=-1, keepdims=True)
    acc = alpha * acc + _dot(p, v)
    return m_new, l, acc


def _flash_init(t):
    return (jnp.full((t, 1), NEG_INF, F32), jnp.zeros((t, 1), F32), jnp.zeros((t, LANES), F32))


def _causal_sweep(step, qi, n_chains, t):
    init = tuple(_flash_init(t) for _ in range(n_chains))
    carry = lax.fori_loop(0, qi, functools.partial(step, diag=False), init)
    return [acc / l for _, l, acc in step(qi, carry, True)]


def _mla_flash_kernel(q_ref, k_ref, v_ref, o_ref):
    t = q_ref.shape[0]
    qi = pl.program_id(2)
    lane = _iota((t, LANES), 1)
    qs = [q_ref[:, hh * LANES:(hh + 1) * LANES].astype(BF16) for hh in range(2)]

    def step(j, carry, diag):
        rows = pl.ds(pl.multiple_of(j * t, t), t)
        kk = k_ref[rows, :].astype(BF16)
        vv = v_ref[rows, :].astype(BF16)
        new = []
        for hh in range(2):
            s = _dot_nt(qs[hh], kk[:, hh * LANES:(hh + 1) * LANES])
            if diag:
                s = jnp.where(_iota((t, t), 1) <= _iota((t, t), 0), s, NEG_INF)
            new.append(_online_step(s, vv, carry[hh]))
        return tuple(new)

    outs = _causal_sweep(step, qi, 2, t)
    o_ref[...] = jnp.where(lane < MLA_VD, outs[0], outs[1])


def _mla_flash(qhat, khat, v, b, t):
    tq = min(MLA_FLASH_T, t)
    nq = t // tq
    return pl.pallas_call(
        _mla_flash_kernel,
        grid=(b, MLA_HEADS // 2, nq),
        in_specs=[pl.BlockSpec((tq, 2 * LANES), lambda i, p, j: (i * nq + j, p)),
                  pl.BlockSpec((t, 2 * LANES), lambda i, p, j: (i, p)),
                  pl.BlockSpec((t, LANES), lambda i, p, j: (i, p))],
        out_specs=pl.BlockSpec((tq, LANES), lambda i, p, j: (i * nq + j, p)),
        out_shape=jax.ShapeDtypeStruct((b * t, SELF_W), F32),
        compiler_params=_params(("parallel", "parallel", "arbitrary")),
        name="mla_flash",
    )(qhat, khat, v)


def _dif_lambda(lp, lam_init):
    a = jnp.sum(lp[0:1] * lp[1:2], axis=-1, keepdims=True)
    b = jnp.sum(lp[2:3] * lp[3:4], axis=-1, keepdims=True)
    return jnp.exp(a) - jnp.exp(b) + lam_init


def _dif_flash_kernel(lam_init, q_ref, k_ref, v_ref, slope_ref, lam_ref, sg_ref, o_ref):
    t = q_ref.shape[0]
    qi = pl.program_id(2)
    row = _iota((t, t), 0)
    col = _iota((t, t), 1)
    causal = col <= row
    dist0 = (row - col).astype(F32)
    lane = _iota((t, LANES), 1)
    lam = _dif_lambda(lam_ref[...], lam_init)
    for g in range(2):
        nsl = -slope_ref[0, g:g + 1, 0:1]
        nb0 = nsl * dist0
        qc = [jnp.where((lane >= c * DIF_HD) & (lane < (c + 1) * DIF_HD),
                        q_ref[:, g * LANES:(g + 1) * LANES], 0.0).astype(BF16) for c in range(2)]

        def step(j, carry, diag, nsl=nsl, nb0=nb0, qc=qc):
            rows = pl.ds(pl.multiple_of(j * t, t), t)
            kk = k_ref[0, 0, rows, :].astype(BF16)
            vv = v_ref[0, 0, rows, :].astype(BF16)
            off = jnp.full((1, 1), (qi - j) * t, jnp.int32).astype(F32)
            new = []
            for c in range(2):
                s = _dot_nt(qc[c], kk) + nb0
                if diag:
                    s = jnp.where(causal, s, NEG_INF)
                new.append(_online_step(s, vv, carry[c], None if diag else nsl * off))
            return tuple(new)

        os = _causal_sweep(step, qi, 2, t)
        a = os[0] - lam * os[1]
        o_ref[:, g * LANES:(g + 1) * LANES] = _rms(a) * sg_ref[...] * (1.0 - lam_init)


def _dif_flash(qn, kn, v, slopes, lam_p, sub_g, lam_init, b, t):
    tq = min(DIF_FLASH_T, t)
    nq = t // tq
    return pl.pallas_call(
        functools.partial(_dif_flash_kernel, lam_init),
        grid=(b, DIF_KV_HEADS, nq),
        in_specs=[pl.BlockSpec((tq, 2 * LANES), lambda i, p, j: (i * nq + j, p)),
                  pl.BlockSpec((1, 1, t, DIF_VD), lambda i, p, j: (i, p, 0, 0)),
                  pl.BlockSpec((1, 1, t, DIF_VD), lambda i, p, j: (i, p, 0, 0)),
                  pl.BlockSpec((1, SUBLANES, LANES), lambda i, p, j: (p, 0, 0)),
                  pl.BlockSpec(lam_p.shape, lambda i, p, j: (0, 0)),
                  pl.BlockSpec(sub_g.shape, lambda i, p, j: (0, 0))],
        out_specs=pl.BlockSpec((tq, 2 * LANES), lambda i, p, j: (i * nq + j, p)),
        out_shape=jax.ShapeDtypeStruct((b * t, SELF_W), F32),
        compiler_params=_params(("parallel", "parallel", "arbitrary")),
        name="dif_flash",
    )(qn, kn, v, slopes, lam_p, sub_g)


def _sample_cross(cq8, mkt, mvt):
    row = _iota((SUBLANES, MEM_W), 0)
    head = jnp.right_shift(_iota((SUBLANES, MEM_W), 1), 6)
    upper = (row >= 4).astype(jnp.int32)
    cqd = jnp.where(row < 4, cq8, pltpu.roll(cq8, 4, 0))
    q16 = jnp.concatenate([jnp.where(head == upper, cqd, 0.0),
                           jnp.where(head == 2 + upper, cqd, 0.0)], axis=0)
    s = _dot(q16, mkt)
    p = jnp.exp(s - jnp.max(s, axis=-1, keepdims=True))
    o = _dot_nt(p, mvt) / jnp.sum(p, axis=-1, keepdims=True)
    o0 = o[0:SUBLANES]
    o1 = o[SUBLANES:2 * SUBLANES]
    t0 = jnp.where(head == 0, o0, pltpu.roll(o0, 4, 0))
    t1 = jnp.where(head == 2, o1, pltpu.roll(o1, 4, 0))
    return jnp.where(head < 2, t0, t1)


def _softmax_update(s, m_sc, l_sc, acc_sc, pv):
    m_prev = m_sc[...]
    m_new = jnp.maximum(m_prev, jnp.max(s, axis=-1, keepdims=True))
    alpha = jnp.exp(m_prev - m_new)
    p = jnp.exp(s - m_new[:, 0:1])
    l_sc[...] = alpha * l_sc[...] + jnp.sum(p, axis=-1, keepdims=True)
    acc_sc[...] = alpha * acc_sc[...] + pv(p)
    m_sc[...] = m_new


def _mla_sample_kernel(pps, pt_ref, *refs):
    pages = refs[:pps]
    lhsq_ref, wt_ref, new_ref, wuv_ref, cq_ref, mk_ref, mv_ref = refs[pps:pps + 7]
    self_ref, cross_ref = refs[pps + 7:pps + 9]
    lhs_sc, m_sc, l_sc, acc_sc, tile_sc, new_sc = refs[pps + 9:]
    del pt_ref
    c = pl.program_id(1)
    nrow = MLA_HEADS * 4
    nk = MLA_HEADS * MLA_NOPE

    @pl.when(c == 0)
    def _():
        lhs_sc[0:nk, :] = wt_ref[...]
        lhs_sc[nk:nk + 2 * nrow, :] = lhsq_ref[0].astype(BF16)
        m_sc[...] = jnp.full(m_sc.shape, NEG_INF, F32)
        l_sc[...] = jnp.zeros(l_sc.shape, F32)
        acc_sc[...] = jnp.zeros(acc_sc.shape, F32)
        tile_sc[...] = jnp.zeros(tile_sc.shape, BF16)
        new_sc[...] = jnp.zeros(new_sc.shape, BF16)
        new_sc[0:MLA_ROW, :] = new_ref[0].astype(BF16)
        cross_ref[0] = _sample_cross(cq_ref[0], mk_ref[0, 0], mv_ref[0, 0])

    def process(tb, is_new):
        w = tb.shape[1]
        r_all = _dot(lhs_sc[...], tb)
        rs = []
        for hd in range(MLA_HEADS):
            kh = r_all[hd * MLA_NOPE:(hd + 1) * MLA_NOPE, :]
            ss = jnp.sum(kh * kh, axis=0, keepdims=True)
            rs.append(lax.rsqrt(ss * (1.0 / MLA_NOPE) + EPS))
        row8 = _iota((SUBLANES, w), 0)
        r48 = jnp.concatenate([jnp.where(row8 < 4, rs[2 * i], rs[2 * i + 1])
                               for i in range(MLA_HEADS // 2)], axis=0)
        s = r_all[nk:nk + nrow, :] * r48 + r_all[nk + nrow:nk + 2 * nrow, :]
        if is_new:
            qpos = _iota((nrow, w), 0) & 3
            s = jnp.where(_iota((nrow, w), 1) <= qpos, s, NEG_INF)
        _softmax_update(s, m_sc, l_sc, acc_sc, lambda p: _dot_nt(p, tb[0:KV_LORA, :]))

    for i in range(pps):
        tile_sc[0:MLA_ROW, i * PAGE:(i + 1) * PAGE] = pages[i][0, 0].astype(BF16)
    process(tile_sc[...], False)

    @pl.when(c == pl.num_programs(1) - 1)
    def _():
        process(new_sc[...], True)
        o_lat = acc_sc[...] / l_sc[...]
        lane = _iota((SUBLANES, LANES), 1)
        for i in range(MLA_HEADS // 2):
            res = _dot(o_lat[i * SUBLANES:(i + 1) * SUBLANES], wuv_ref[:, i * LANES:(i + 1) * LANES])
            self_ref[0, :, i * LANES:(i + 1) * LANES] = jnp.where(lane < MLA_VD, res, pltpu.roll(res, 4, 0))


def _page_spec(page_shape, layer_j, n_pages, pps, i):
    def index(s, c, pt):
        return (pt[s * n_pages + c * pps + i], layer_j) + (0,) * len(page_shape)
    return pl.BlockSpec((1, 1) + page_shape, index)


def _mla_sample(page_table, cache, layer_j, lhsq, w_t, rows_new, w_uv, cq8, memk, memv, layer_i):
    db, n_pages = page_table.shape
    pps = min(PAGES_PER_STEP, n_pages)
    nc = n_pages // pps
    nrow = MLA_HEADS * 4
    seq = lambda shape: pl.BlockSpec((1,) + shape, lambda s, c, pt: (s,) + (0,) * len(shape))
    const = lambda shape: pl.BlockSpec(shape, lambda s, c, pt: (0,) * len(shape))
    mem = pl.BlockSpec((1, 1, MEM_LEN, MEM_W), lambda s, c, pt: (s, layer_i, 0, 0))
    grid_spec = pltpu.PrefetchScalarGridSpec(
        num_scalar_prefetch=1,
        grid=(db, nc),
        in_specs=[_page_spec((MLA_ROW, PAGE), layer_j, n_pages, pps, i) for i in range(pps)] + [
            seq((2 * nrow, ROW_PAD)), const(w_t.shape), seq((MLA_ROW, LANES)), const(w_uv.shape),
            seq((SUBLANES, MEM_W)), mem, mem],
        out_specs=[seq((SUBLANES, SELF_W)), seq((SUBLANES, MEM_W))],
        scratch_shapes=[pltpu.VMEM((MLA_HEADS * MLA_NOPE + 2 * nrow, ROW_PAD), BF16),
                        pltpu.VMEM((nrow, LANES), F32), pltpu.VMEM((nrow, LANES), F32),
                        pltpu.VMEM((nrow, LANES), F32), pltpu.VMEM((ROW_PAD, pps * PAGE), BF16),
                        pltpu.VMEM((ROW_PAD, LANES), BF16)])
    return pl.pallas_call(
        functools.partial(_mla_sample_kernel, pps),
        grid_spec=grid_spec,
        out_shape=[jax.ShapeDtypeStruct((db, SUBLANES, SELF_W), F32),
                   jax.ShapeDtypeStruct((db, SUBLANES, MEM_W), F32)],
        compiler_params=_params(("arbitrary", "arbitrary")),
        name="mla_sample",
    )(page_table.reshape(-1), *([cache] * pps), lhsq, w_t, rows_new, w_uv, cq8, memk, memv)


def _dif_sample_kernel(pps, lam_init, past, pt_ref, *refs):
    kpages = refs[:pps]
    vpages = refs[pps:2 * pps]
    (lhsq_ref, knew_ref, vnew_ref, nsl_ref, lam_ref, sg_ref,
     cq_ref, mk_ref, mv_ref) = refs[2 * pps:2 * pps + 9]
    self_ref, cross_ref = refs[2 * pps + 9:2 * pps + 11]
    m_sc, l_sc, acc_sc, knew_sc, vnew_sc = refs[2 * pps + 11:]
    del pt_ref
    c = pl.program_id(1)
    nrow = DIF_KV_HEADS * 16

    @pl.when(c == 0)
    def _():
        m_sc[...] = jnp.full(m_sc.shape, NEG_INF, F32)
        l_sc[...] = jnp.zeros(l_sc.shape, F32)
        acc_sc[...] = jnp.zeros(acc_sc.shape, F32)
        knew_sc[...] = jnp.zeros(knew_sc.shape, F32)
        vnew_sc[...] = jnp.zeros(vnew_sc.shape, F32)
        knew_sc[0:SUBLANES, :] = knew_ref[0]
        vnew_sc[0:SUBLANES, :] = vnew_ref[0]
        cross_ref[0] = _sample_cross(cq_ref[0], mk_ref[0, 0], mv_ref[0, 0])

    def process(kts, vts, base, is_new):
        q = lhsq_ref[0]
        s = jnp.concatenate(
            [jnp.concatenate([_dot_nt(q[kv * 16:(kv + 1) * 16], kt) for kt in kts[kv]], axis=1)
             for kv in range(DIF_KV_HEADS)], axis=0)
        w = s.shape[1]
        qrow = _iota((nrow, w), 0) & 3
        col = _iota((nrow, w), 1)
        dist = (past + qrow - base - col).astype(F32)
        s = s + nsl_ref[:, 0:1] * dist
        if is_new:
            s = jnp.where(col <= qrow, s, NEG_INF)

        def pv(p):
            outs = []
            for kv in range(DIF_KV_HEADS):
                o = 0
                for ti, vt in enumerate(vts[kv]):
                    rows = vt.shape[0]
                    o = o + _dot(p[kv * 16:(kv + 1) * 16, ti * rows:(ti + 1) * rows], vt)
                outs.append(o)
            return jnp.concatenate(outs, axis=0)
        _softmax_update(s, m_sc, l_sc, acc_sc, pv)

    def tiles(pg):
        return [[jnp.concatenate([pg[2 * i][0, 0, kv], pg[2 * i + 1][0, 0, kv]], axis=0)
                 for i in range(pps // 2)] for kv in range(DIF_KV_HEADS)]

    process(tiles(kpages), tiles(vpages), c * pps * PAGE, False)

    @pl.when(c == pl.num_programs(1) - 1)
    def _():
        heads = lambda ref: [[ref[:, kv * LANES:(kv + 1) * LANES]] for kv in range(DIF_KV_HEADS)]
        process(heads(knew_sc), heads(vnew_sc), past, True)
        a = acc_sc[...] / l_sc[...]
        lam = _dif_lambda(lam_ref[...], lam_init)
        for i in range(DIF_HEADS):
            an = a[i * SUBLANES:(i + 1) * SUBLANES]
            o = an - lam * pltpu.roll(an, 4, 0)
            self_ref[0, :, i * LANES:(i + 1) * LANES] = _rms(o) * sg_ref[...] * (1.0 - lam_init)


def _dif_sample(page_table, cache_k, cache_v, layer_j, lhsq, k_new, v_new, nsl, lam_p, sub_g,
                lam_init, cq8, memk, memv, layer_i):
    db, n_pages = page_table.shape
    pps = min(PAGES_PER_STEP, n_pages)
    nc = n_pages // pps
    nrow = DIF_KV_HEADS * 16
    seq = lambda shape: pl.BlockSpec((1,) + shape, lambda s, c, pt: (s,) + (0,) * len(shape))
    const = lambda shape: pl.BlockSpec(shape, lambda s, c, pt: (0,) * len(shape))
    mem = pl.BlockSpec((1, 1, MEM_LEN, MEM_W), lambda s, c, pt: (s, layer_i, 0, 0))
    pages = [_page_spec((DIF_KV_HEADS, PAGE, DIF_VD), layer_j, n_pages, pps, i) for i in range(pps)]
    grid_spec = pltpu.PrefetchScalarGridSpec(
        num_scalar_prefetch=1,
        grid=(db, nc),
        in_specs=pages + pages + [
            seq((nrow, LANES)), seq((SUBLANES, DIF_KW)), seq((SUBLANES, DIF_KW)), const(nsl.shape),
            const(lam_p.shape), const(sub_g.shape), seq((SUBLANES, MEM_W)), mem, mem],
        out_specs=[seq((SUBLANES, SELF_W)), seq((SUBLANES, MEM_W))],
        scratch_shapes=[pltpu.VMEM((nrow, LANES), F32), pltpu.VMEM((nrow, LANES), F32),
                        pltpu.VMEM((nrow, LANES), F32), pltpu.VMEM((PAGE, DIF_KW), F32),
                        pltpu.VMEM((PAGE, DIF_KW), F32)])
    return pl.pallas_call(
        functools.partial(_dif_sample_kernel, pps, lam_init, n_pages * PAGE),
        grid_spec=grid_spec,
        out_shape=[jax.ShapeDtypeStruct((db, SUBLANES, SELF_W), F32),
                   jax.ShapeDtypeStruct((db, SUBLANES, MEM_W), F32)],
        compiler_params=_params(("arbitrary", "arbitrary")),
        name="dif_sample",
    )(page_table.reshape(-1), *([cache_k] * pps), *([cache_v] * pps), lhsq, k_new, v_new, nsl,
      lam_p, sub_g, cq8, memk, memv)


def _out_kernel(with_cross, x_ref, self_ref, c_ref, gs_ref, *rest):
    if with_cross:
        mk_ref, mv_ref, w_ref, y_ref = rest
        tm = x_ref.shape[0]
        cq = c_ref[...]
        mk = mk_ref[0, 0].astype(BF16)
        mv = mv_ref[0, 0].astype(BF16)
        head = jnp.right_shift(_iota((tm, MEM_W), 1), 6)
        cross = jnp.zeros((tm, MEM_W), F32)
        for hd in range(MEM_HEADS):
            s = _dot_nt(jnp.where(head == hd, cq, 0.0), mk)
            p = jnp.exp(s - jnp.max(s, axis=-1, keepdims=True))
            o = _dot(p, mv) / jnp.sum(p, axis=-1, keepdims=True)
            cross = jnp.where(head == hd, o, cross)
    else:
        w_ref, y_ref = rest
        cross = c_ref[...]
    mix = jnp.concatenate([self_ref[...], cross], axis=-1) * gs_ref[...]
    y_ref[...] = x_ref[...] + _dot(mix, w_ref[...])


def _out_proj(x, self_o, c, gs, w, memk=None, memv=None, layer_i=0, tokens_per_batch=None):
    n = x.shape[0]
    tm = min(PROJ_TM, n)
    tok = lambda wd: pl.BlockSpec((tm, wd), lambda i: (i, 0))
    ins = [x, self_o, c, gs]
    specs = [tok(D_MODEL), tok(SELF_W), tok(MEM_W), tok(MIX_W)]
    with_cross = memk is not None
    if with_cross:
        per = tokens_per_batch // tm
        mem = pl.BlockSpec((1, 1, MEM_LEN, MEM_W), lambda i: (i // per, layer_i, 0, 0))
        ins += [memk, memv]
        specs += [mem, mem]
    ins.append(w)
    specs.append(_full(w.shape))
    return pl.pallas_call(
        functools.partial(_out_kernel, with_cross),
        grid=(n // tm,),
        in_specs=specs,
        out_specs=tok(D_MODEL),
        out_shape=jax.ShapeDtypeStruct((n, D_MODEL), F32),
        compiler_params=_params(("parallel",)),
        name="out_proj_prompt" if with_cross else "out_proj_sample",
    )(*ins)


def _rope_tables(pos, g_nope, g_rope, scale):
    half = MLA_ROPE // 2
    inv_freq = ROPE_THETA ** (-jnp.arange(half, dtype=F32) / half)
    ang = pos.astype(F32)[:, None] * inv_freq
    c, s = jnp.cos(ang), jnp.sin(ang)
    g1, g2 = g_rope[:half].astype(F32), g_rope[half:].astype(F32)
    n = pos.shape[0]
    z32 = jnp.zeros((n, LANES - MLA_NOPE - MLA_ROPE), F32)
    a = jnp.concatenate([jnp.broadcast_to(g_nope.astype(F32), (n, MLA_NOPE)), g1 * c, g2 * c, z32], axis=1)
    b = jnp.concatenate([jnp.zeros((n, MLA_NOPE), F32), -g2 * s, g1 * s, z32], axis=1)
    return a * scale, b * scale


def _row(v):
    return v.astype(F32).reshape(1, -1)


def _mla_weights(i, j, norm_g, cq_norm_g, mla_w_in, mla_q_a_norm_g, mla_w_q_up, mla_kv_a_norm_g,
                 mla_w_uk, mla_w_uv, mla_kn_g):
    w = mla_w_in[j]
    c0 = Q_LORA + KV_LORA
    x1 = w[:, c0:c0 + 16]
    x2 = w[:, c0 + 16:c0 + 32]
    w_in = jnp.concatenate([w[:, :c0], jnp.zeros((D_MODEL, 64), w.dtype), x1, x2, x2, x1,
                            w[:, c0 + 32:]], axis=1).astype(BF16)
    wq = mla_w_q_up[j].reshape(Q_LORA, MLA_HEADS, MLA_NOPE + MLA_ROPE)
    qn, q1, q2 = wq[..., :MLA_NOPE], wq[..., MLA_NOPE:MLA_NOPE + 16], wq[..., MLA_NOPE + 16:]
    w_q = jnp.concatenate([qn, q1, q2, q2, q1], axis=-1).reshape(Q_LORA, MLA_HEADS * LANES).astype(BF16)
    wuk = mla_w_uk[j].reshape(KV_LORA, MLA_HEADS, MLA_NOPE)
    w_uk_pad = jnp.pad(wuk, ((0, 0), (0, 0), (0, LANES - MLA_NOPE))).reshape(KV_LORA, -1).astype(BF16)
    w_t = jnp.pad(mla_w_uk[j].T, ((0, 0), (0, ROW_PAD - KV_LORA))).astype(BF16)
    wabs = jnp.transpose(wuk * mla_kn_g[j].astype(F32), (1, 2, 0))
    w_abs = jnp.pad(wabs, ((0, 0), (0, LANES - MLA_NOPE), (0, 0))).astype(BF16)
    kn_g = _row(jnp.pad(mla_kn_g[j], (0, LANES - MLA_NOPE)))
    return dict(g=_row(norm_g[i]), w_in=w_in, q_a_g=_row(mla_q_a_norm_g[j]), w_q=w_q,
                kv_a_g=_row(mla_kv_a_norm_g[j]), w_uk_pad=w_uk_pad, w_t=w_t, w_abs=w_abs, kn_g=kn_g,
                w_uv=mla_w_uv[j].astype(BF16),
                cq_g=_row(jnp.tile(cq_norm_g[i], 2)) * MEM_HD ** -0.5)


def kernel(x_prompt, x_sample, cache_mla_kv, cache_diff_k, cache_diff_v, cache_mem_k, cache_mem_v,
           page_table, mem_prompt, norm_g, w_out, mem_norm_g, w_mem_kv, cq_norm_g, ck_norm_g,
           mla_w_in, mla_q_a_norm_g, mla_w_q_up, mla_kv_a_norm_g, mla_w_uk, mla_w_uv,
           mla_qn_g, mla_kn_g, mla_qr_g, mla_kr_g,
           dif_w_in, dif_q_g, dif_k_g, dif_lambda, dif_subln_g):
    b, t, d = x_prompt.shape
    db, ds, _ = x_sample.shape
    n_pages = page_table.shape[1]
    past = n_pages * PAGE
    depth = norm_g.shape[0]
    assert d == D_MODEL and ds == 4 and t % max(PROJ_TM, MLA_FLASH_T, DIF_FLASH_T) == 0
    assert n_pages % min(PAGES_PER_STEP, n_pages) == 0 and min(PAGES_PER_STEP, n_pages) % 2 == 0

    pos_p = jnp.arange(t, dtype=jnp.int32)
    pos_s = jnp.tile(past + jnp.arange(ds, dtype=jnp.int32), db)
    xp = x_prompt.reshape(b * t, d)
    xs = x_sample.reshape(db * ds, d)
    w_out_b = w_out.astype(BF16)

    memk, memv = _mem_kv(mem_prompt, mem_norm_g.astype(F32).reshape(depth, 1, d), w_mem_kv.astype(BF16),
                         jnp.tile(ck_norm_g.astype(F32), (1, 2)).reshape(depth, 1, LANES))
    smem_k = jnp.transpose(cache_mem_k, (0, 1, 3, 4, 2)).reshape(db, depth, MEM_W, MEM_LEN)
    smem_v = jnp.transpose(cache_mem_v, (0, 1, 3, 4, 2)).reshape(db, depth, MEM_W, MEM_LEN)
    cache_mla_t = jnp.swapaxes(cache_mla_kv, 2, 3)
    cache_dk = jnp.transpose(cache_diff_k, (0, 1, 3, 2, 4))
    cache_dv = jnp.transpose(cache_diff_v, (0, 1, 3, 2, 4))

    def pad8(a, k=1):
        return jnp.pad(a.reshape(db, ds, -1), ((0, 0), (0, k * SUBLANES - ds), (0, 0)))

    mla_rows_p, mla_rows_s, dk_p, dv_p, dk_s, dv_s = [], [], [], [], [], []
    slopes = 2.0 ** (-8.0 * jnp.arange(1, DIF_HEADS + 1, dtype=F32) / DIF_HEADS)

    for i in range(depth):
        j = i // 2
        if i % 2 == 0:
            lw = _mla_weights(i, j, norm_g, cq_norm_g, mla_w_in, mla_q_a_norm_g, mla_w_q_up,
                              mla_kv_a_norm_g, mla_w_uk, mla_w_uv, mla_kn_g)
            scale = (MLA_NOPE + MLA_ROPE) ** -0.5
            tabs_p = (*_rope_tables(pos_p, mla_qn_g[j], mla_qr_g[j], scale),
                      *_rope_tables(pos_p, jnp.zeros((MLA_NOPE,), F32), mla_kr_g[j], 1.0))
            tabs_s = (*_rope_tables(pos_s, mla_qn_g[j], mla_qr_g[j], scale),
                      *_rope_tables(pos_s, jnp.zeros((MLA_NOPE,), F32), mla_kr_g[j], 1.0))
            qhat, khat, v_p, rows_p, cq_p, gs_p = _mla_proj(xp, tabs_p, lw, False)
            qabs, qrope, rows_s, cq_s, gs_s = _mla_proj(xs, tabs_s, lw, True)
            self_p = _mla_flash(qhat, khat, v_p, b, t)

            def to_rows(a):
                return a.reshape(db, ds, MLA_HEADS, LANES).transpose(0, 2, 1, 3).reshape(db, MLA_HEADS * ds, LANES)
            lhs_n = jnp.pad(to_rows(qabs), ((0, 0), (0, 0), (0, ROW_PAD - KV_LORA)))
            lhs_r = jnp.pad(to_rows(qrope), ((0, 0), (0, 0), (KV_LORA, 0)))
            lhsq = jnp.concatenate([lhs_n, lhs_r], axis=1)
            new_t = jnp.pad(jnp.swapaxes(rows_s.reshape(db, ds, MLA_ROW), 1, 2),
                            ((0, 0), (0, 0), (0, LANES - ds)))
            self_s, cross_s = _mla_sample(page_table, cache_mla_t, j, lhsq, lw["w_t"], new_t,
                                          lw["w_uv"], pad8(cq_s), smem_k, smem_v, i)
            mla_rows_p.append(rows_p.reshape(b, t, MLA_ROW))
            mla_rows_s.append(rows_s.reshape(db, ds, MLA_ROW))
        else:
            lam_init = 0.8 - 0.6 * math.exp(-0.3 * i)
            lw = dict(g=_row(norm_g[i]), w_in=dif_w_in[j].astype(BF16),
                      q_g=_row(jnp.tile(dif_q_g[j], 2)) * DIF_HD ** -0.5, k_g=_row(jnp.tile(dif_k_g[j], 2)),
                      cq_g=_row(jnp.tile(cq_norm_g[i], 2)) * MEM_HD ** -0.5)
            lam_p = dif_lambda[j].astype(F32)
            sub_g = _row(dif_subln_g[j])
            qn_p, kn_p, v_p, cq_p, gs_p = _dif_proj(xp, lw, b)
            qn_s, kn_s, v_s, cq_s, gs_s = _dif_proj(xs, lw, 1)
            to_tok = lambda a: jnp.transpose(a[0], (1, 0, 2)).reshape(db * ds, DIF_KW)
            kn_s, v_s = to_tok(kn_s), to_tok(v_s)
            slope_tab = jnp.broadcast_to(
                jnp.pad(slopes.reshape(DIF_KV_HEADS, 2), ((0, 0), (0, SUBLANES - 2)))[:, :, None],
                (DIF_KV_HEADS, SUBLANES, LANES))
            self_p = _dif_flash(qn_p, kn_p, v_p, slope_tab, lam_p, sub_g, lam_init, b, t)

            q5 = qn_s.reshape(db, ds, DIF_KV_HEADS, 2, 1, LANES).transpose(0, 2, 3, 4, 1, 5)
            lane_map = (jnp.arange(LANES) // DIF_HD)[None, None, None, None, None, :]
            q6 = jnp.where(lane_map == jnp.arange(2)[None, None, None, :, None, None], q5, 0.0)
            lhsq = q6.reshape(db, DIF_KV_HEADS * 16, LANES)
            nsl = jnp.broadcast_to(-jnp.repeat(slopes, 8)[:, None], (DIF_KV_HEADS * 16, LANES))
            self_s, cross_s = _dif_sample(page_table, cache_dk, cache_dv, j, lhsq, pad8(kn_s), pad8(v_s),
                                          nsl, lam_p, sub_g, lam_init, pad8(cq_s), smem_k, smem_v, i)
            dk_p.append(kn_p)
            dv_p.append(v_p)
            dk_s.append(kn_s.reshape(db, ds, DIF_KV_HEADS, DIF_VD))
            dv_s.append(v_s.reshape(db, ds, DIF_KV_HEADS, DIF_VD))

        xp = _out_proj(xp, self_p, cq_p, gs_p, w_out_b[i], memk, memv, i, t)
        xs = _out_proj(xs, self_s[:, :ds].reshape(db * ds, SELF_W), cross_s[:, :ds].reshape(db * ds, MEM_W),
                       gs_s, w_out_b[i])

    mem_shape = (b, depth, MEM_LEN, MEM_HEADS, MEM_HD)
    tok_major = lambda parts: jnp.transpose(jnp.stack(parts, axis=1), (0, 1, 3, 2, 4))
    return (xp.reshape(b, t, d), xs.reshape(db, ds, d),
            jnp.stack(mla_rows_p, axis=1), tok_major(dk_p), tok_major(dv_p),
            memk.reshape(mem_shape), memv.reshape(mem_shape),
            jnp.stack(mla_rows_s, axis=1), jnp.stack(dk_s, axis=1), jnp.stack(dv_s, axis=1))
```

```python
import functools
import math

import jax
import jax.numpy as jnp
from jax import lax
from jax.experimental import pallas as pl
from jax.experimental.pallas import tpu as pltpu

F32 = jnp.float32
BF16 = jnp.bfloat16

LANES = 128
SUBLANES = 8
VMEM_LIMIT = 56 * 1024 * 1024

D_MODEL = 1024
PAGE = 128
MEM_LEN = 256
MEM_HEADS = 4
MEM_HD = 64
MEM_W = MEM_HEADS * MEM_HD
MIX_W = D_MODEL
SELF_W = MIX_W - MEM_W
MLA_HEADS = 12
MLA_NOPE = 64
MLA_ROPE = 32
MLA_VD = 64
Q_LORA = 256
KV_LORA = 128
MLA_ROW = KV_LORA + MLA_ROPE
ROW_PAD = 2 * LANES
ROPE_THETA = 10000.0
DIF_HEADS = 6
DIF_KV_HEADS = 3
DIF_HD = 64
DIF_VD = 2 * DIF_HD
DIF_QW = DIF_HEADS * DIF_VD
DIF_KW = DIF_KV_HEADS * DIF_VD
EPS = 1e-6
NEG_INF = float("-inf")

PROJ_TM = 512
MLA_FLASH_T = 512
DIF_FLASH_T = 512
PAGES_PER_STEP = 16


def _dot(a, b):
    return jnp.dot(a.astype(BF16), b.astype(BF16), preferred_element_type=F32)


def _dot_nt(a, b):
    return lax.dot_general(a.astype(BF16), b.astype(BF16), (((1,), (1,)), ((), ())),
                           preferred_element_type=F32)


def _iota(shape, dim):
    return lax.broadcasted_iota(jnp.int32, shape, dim)


def _rms(x):
    return x * lax.rsqrt(jnp.mean(x * x, axis=-1, keepdims=True) + EPS)


def _seg_rsqrt(sq, lane, lo, hi):
    ss = jnp.sum(jnp.where((lane >= lo) & (lane < hi), sq, 0.0), axis=-1, keepdims=True)
    return lax.rsqrt(ss * (1.0 / (hi - lo)) + EPS)


def _norm_halves(blk, lane):
    sq = blk * blk
    r0 = _seg_rsqrt(sq, lane, 0, 64)
    r1 = _seg_rsqrt(sq, lane, 64, 128)
    return blk * jnp.where(lane < 64, r0, r1)


def _silu(x):
    return x / (1.0 + jnp.exp(-x))


def _params(sem):
    return pltpu.CompilerParams(dimension_semantics=sem, vmem_limit_bytes=VMEM_LIMIT)


def _cross_query_and_gate(z_cq, z_gate, cqg, cqn_ref, gs_ref, lane):
    for p in range(MEM_W // LANES):
        blk = z_cq[:, p * LANES:(p + 1) * LANES]
        cqn_ref[:, p * LANES:(p + 1) * LANES] = _norm_halves(blk, lane) * cqg
    gs_ref[...] = _silu(z_gate)


def _mla_proj_kernel(sample, x_ref, g_ref, win_ref, qag_ref, wq_ref, kvg_ref,
                     qa_ref, qb_ref, ka_ref, kb_ref, cqg_ref, *rest):
    if sample:
        wabs_ref, qabs_ref, qrope_ref, rows_ref, cqn_ref, gs_ref = rest
    else:
        wuk_ref, kng_ref, wuv_ref, qhat_ref, khat_ref, v_ref, rows_ref, cqn_ref, gs_ref = rest
    tm = x_ref.shape[0]
    lane = _iota((tm, LANES), 1)

    h = _rms(x_ref[...]) * g_ref[...]
    z = _dot(h, win_ref[...])
    z_q = z[:, 0:256]
    z_c = z[:, 256:384]
    z_r = z[:, 384:512]
    _cross_query_and_gate(z[:, 512:768], z[:, 768:1792], cqg_ref[...], cqn_ref, gs_ref, lane)

    q = _dot(_rms(z_q) * qag_ref[...], wq_ref[...])
    qa = qa_ref[...]
    qb = qb_ref[...]
    for hd in range(MLA_HEADS):
        blk = q[:, hd * LANES:(hd + 1) * LANES]
        sq = blk * blk
        r = jnp.where(lane < 64, _seg_rsqrt(sq, lane, 0, 64), _seg_rsqrt(sq, lane, 64, 96))
        qh = (blk * qa + pltpu.roll(blk, 96, 1) * qb) * r
        if sample:
            qabs_ref[:, hd * LANES:(hd + 1) * LANES] = _dot(qh, wabs_ref[hd])
            qrope_ref[:, hd * LANES:(hd + 1) * LANES] = jnp.where(
                lane < MLA_ROPE, pltpu.roll(qh, 64, 1), 0.0)
        else:
            qhat_ref[:, hd * LANES:(hd + 1) * LANES] = qh

    lat = _rms(z_c) * kvg_ref[...]
    rk = _seg_rsqrt(z_r * z_r, lane, 64, 96)
    kr = (z_r * ka_ref[...] + pltpu.roll(z_r, 96, 1) * kb_ref[...]) * rk
    rows_ref[:, 0:KV_LORA] = lat
    rows_ref[:, KV_LORA:MLA_ROW] = pltpu.roll(kr, 64, 1)[:, 0:MLA_ROPE]

    if not sample:
        kx = _dot(lat, wuk_ref[...])
        kng = kng_ref[...]
        for hd in range(MLA_HEADS):
            blk = kx[:, hd * LANES:(hd + 1) * LANES]
            r = _seg_rsqrt(blk * blk, lane, 0, 64)
            khat_ref[:, hd * LANES:(hd + 1) * LANES] = blk * r * kng + kr
        v_ref[...] = _dot(lat, wuv_ref[...])


def _dif_proj_kernel(x_ref, g_ref, win_ref, qg_ref, kg_ref, cqg_ref,
                     qn_ref, kn_ref, v_ref, cqn_ref, gs_ref):
    tm = x_ref.shape[0]
    lane = _iota((tm, LANES), 1)
    h = _rms(x_ref[...]) * g_ref[...]
    z = _dot(h, win_ref[...])
    o = DIF_QW + 2 * DIF_KW
    _cross_query_and_gate(z[:, o:o + MEM_W], z[:, o + MEM_W:], cqg_ref[...], cqn_ref, gs_ref, lane)
    qg = qg_ref[...]
    for hd in range(DIF_HEADS):
        qn_ref[:, hd * LANES:(hd + 1) * LANES] = _norm_halves(z[:, hd * LANES:(hd + 1) * LANES], lane) * qg
    kg = kg_ref[...]
    for hd in range(DIF_KV_HEADS):
        blk = z[:, DIF_QW + hd * LANES:DIF_QW + (hd + 1) * LANES]
        kn_ref[0, hd] = _norm_halves(blk, lane) * kg
        v_ref[0, hd] = z[:, DIF_QW + DIF_KW + hd * LANES:DIF_QW + DIF_KW + (hd + 1) * LANES]


def _full(shape):
    nd = len(shape)
    return pl.BlockSpec(shape, lambda i: (0,) * nd)


def _mla_proj(x, pos_tabs, lw, sample):
    n = x.shape[0]
    tm = min(PROJ_TM, n)
    npos = pos_tabs[0].shape[0] // tm
    tok = lambda w: pl.BlockSpec((tm, w), lambda i: (i, 0))
    pos = pl.BlockSpec((tm, LANES), lambda i: (i % npos, 0))
    ins = [x, lw["g"], lw["w_in"], lw["q_a_g"], lw["w_q"], lw["kv_a_g"], *pos_tabs, lw["cq_g"]]
    specs = [tok(D_MODEL), _full(lw["g"].shape), _full(lw["w_in"].shape), _full(lw["q_a_g"].shape),
             _full(lw["w_q"].shape), _full(lw["kv_a_g"].shape), pos, pos, pos, pos,
             _full(lw["cq_g"].shape)]
    hw = MLA_HEADS * LANES
    if sample:
        ins += [lw["w_abs"]]
        specs += [_full(lw["w_abs"].shape)]
        outs = [(hw, tok(hw)), (hw, tok(hw))]
    else:
        ins += [lw["w_uk_pad"], lw["kn_g"], lw["w_uv"]]
        specs += [_full(lw["w_uk_pad"].shape), _full(lw["kn_g"].shape), _full(lw["w_uv"].shape)]
        outs = [(hw, tok(hw)), (hw, tok(hw)), (SELF_W, tok(SELF_W))]
    outs += [(MLA_ROW, tok(MLA_ROW)), (MEM_W, tok(MEM_W)), (MIX_W, tok(MIX_W))]
    return pl.pallas_call(
        functools.partial(_mla_proj_kernel, sample),
        grid=(n // tm,),
        in_specs=specs,
        out_specs=[s for _, s in outs],
        out_shape=[jax.ShapeDtypeStruct((n, w), F32) for w, _ in outs],
        compiler_params=_params(("parallel",)),
        name="mla_proj_sample" if sample else "mla_proj_prompt",
    )(*ins)


def _dif_proj(x, lw, nb):
    n = x.shape[0]
    tpb = n // nb
    tm = min(PROJ_TM, tpb)
    per = tpb // tm
    tok = lambda w: pl.BlockSpec((tm, w), lambda i: (i, 0))
    kv = pl.BlockSpec((1, DIF_KV_HEADS, tm, DIF_VD), lambda i: (i // per, 0, i % per, 0))
    kv_shape = jax.ShapeDtypeStruct((nb, DIF_KV_HEADS, tpb, DIF_VD), F32)
    ins = [x, lw["g"], lw["w_in"], lw["q_g"], lw["k_g"], lw["cq_g"]]
    specs = [tok(D_MODEL)] + [_full(a.shape) for a in ins[1:]]
    flat = lambda w: jax.ShapeDtypeStruct((n, w), F32)
    return pl.pallas_call(
        _dif_proj_kernel,
        grid=(n // tm,),
        in_specs=specs,
        out_specs=[tok(DIF_QW), kv, kv, tok(MEM_W), tok(MIX_W)],
        out_shape=[flat(DIF_QW), kv_shape, kv_shape, flat(MEM_W), flat(MIX_W)],
        compiler_params=_params(("parallel",)),
        name="dif_proj",
    )(*ins)


def _mem_kv_kernel(mem_ref, g_ref, w_ref, ckg_ref, mk_ref, mv_ref):
    m = mem_ref.shape[1]
    lane = _iota((m, LANES), 1)
    kv = _dot(_rms(mem_ref[0]) * g_ref[0], w_ref[0])
    ckg = ckg_ref[0]
    for p in range(MEM_W // LANES):
        mk_ref[0, 0, :, p * LANES:(p + 1) * LANES] = _norm_halves(kv[:, p * LANES:(p + 1) * LANES], lane) * ckg
    mv_ref[0, 0] = kv[:, MEM_W:]


def _mem_kv(mem, g, w, ckg):
    b, m, _ = mem.shape
    depth = g.shape[0]
    out = pl.BlockSpec((1, 1, m, MEM_W), lambda l, i: (i, l, 0, 0))
    return pl.pallas_call(
        _mem_kv_kernel,
        grid=(depth, b),
        in_specs=[pl.BlockSpec((1, m, D_MODEL), lambda l, i: (i, 0, 0)),
                  pl.BlockSpec((1, 1, D_MODEL), lambda l, i: (l, 0, 0)),
                  pl.BlockSpec((1, D_MODEL, 2 * MEM_W), lambda l, i: (l, 0, 0)),
                  pl.BlockSpec((1, 1, LANES), lambda l, i: (l, 0, 0))],
        out_specs=[out, out],
        out_shape=[jax.ShapeDtypeStruct((b, depth, m, MEM_W), F32)] * 2,
        compiler_params=_params(("parallel", "parallel")),
        name="mem_kv",
    )(mem, g, w, ckg)


def _online_step(s, v, carry):
    m, l, acc = carry
    m_new = jnp.maximum(m, jnp.max(s, axis=-1, keepdims=True))
    alpha = jnp.exp(m - m_new)
    p = jnp.exp(s - m_new)
    l = alpha * l + jnp.sum(p, axis=-1, keepdims=True)
    acc = alpha * acc + _dot(p, v)
    return m_new, l, acc


def _flash_init(t):
    return (jnp.full((t, 1), NEG_INF, F32), jnp.zeros((t, 1), F32), jnp.zeros((t, LANES), F32))


def _causal_sweep(step, qi, n_chains, t):
    init = tuple(_flash_init(t) for _ in range(n_chains))
    carry = lax.fori_loop(0, qi, functools.partial(step, diag=False), init)
    return [acc / l for _, l, acc in step(qi, carry, True)]


def _mla_flash_kernel(q_ref, k_ref, v_ref, o_ref):
    t = q_ref.shape[0]
    qi = pl.program_id(2)
    lane = _iota((t, LANES), 1)
    qs = [q_ref[:, hh * LANES:(hh + 1) * LANES].astype(BF16) for hh in range(2)]

    def step(j, carry, diag):
        rows = pl.ds(pl.multiple_of(j * t, t), t)
        kk = k_ref[rows, :].astype(BF16)
        vv = v_ref[rows, :].astype(BF16)
        new = []
        for hh in range(2):
            s = _dot_nt(qs[hh], kk[:, hh * LANES:(hh + 1) * LANES])
            if diag:
                s = jnp.where(_iota((t, t), 1) <= _iota((t, t), 0), s, NEG_INF)
            new.append(_online_step(s, vv, carry[hh]))
        return tuple(new)

    outs = _causal_sweep(step, qi, 2, t)
    o_ref[...] = jnp.where(lane < MLA_VD, outs[0], outs[1])


def _mla_flash(qhat, khat, v, b, t):
    tq = min(MLA_FLASH_T, t)
    nq = t // tq
    return pl.pallas_call(
        _mla_flash_kernel,
        grid=(b, MLA_HEADS // 2, nq),
        in_specs=[pl.BlockSpec((tq, 2 * LANES), lambda i, p, j: (i * nq + j, p)),
                  pl.BlockSpec((t, 2 * LANES), lambda i, p, j: (i, p)),
                  pl.BlockSpec((t, LANES), lambda i, p, j: (i, p))],
        out_specs=pl.BlockSpec((tq, LANES), lambda i, p, j: (i * nq + j, p)),
        out_shape=jax.ShapeDtypeStruct((b * t, SELF_W), F32),
        compiler_params=_params(("parallel", "parallel", "arbitrary")),
        name="mla_flash",
    )(qhat, khat, v)


def _dif_lambda(lp, lam_init):
    a = jnp.sum(lp[0:1] * lp[1:2], axis=-1, keepdims=True)
    b = jnp.sum(lp[2:3] * lp[3:4], axis=-1, keepdims=True)
    return jnp.exp(a) - jnp.exp(b) + lam_init


def _dif_flash_kernel(lam_init, q_ref, k_ref, v_ref, slope_ref, lam_ref, sg_ref, o_ref):
    t = q_ref.shape[0]
    qi = pl.program_id(2)
    row = _iota((t, t), 0)
    col = _iota((t, t), 1)
    causal = col <= row
    lane = _iota((t, LANES), 1)
    lam = _dif_lambda(lam_ref[...], lam_init)
    kcol = _iota((1, t), 1)
    for g in range(2):
        slope = slope_ref[pl.program_id(1), g]
        qc = [jnp.where((lane >= c * DIF_HD) & (lane < (c + 1) * DIF_HD),
                        q_ref[:, g * LANES:(g + 1) * LANES], 0.0).astype(BF16) for c in range(2)]

        def step(j, carry, diag, slope=slope, qc=qc):
            rows = pl.ds(pl.multiple_of(j * t, t), t)
            kk = k_ref[0, 0, rows, :].astype(BF16)
            vv = v_ref[0, 0, rows, :].astype(BF16)
            nb = (kcol + (j - qi) * t).astype(F32) * slope
            new = []
            for c in range(2):
                s = _dot_nt(qc[c], kk) + nb
                if diag:
                    s = jnp.where(causal, s, NEG_INF)
                new.append(_online_step(s, vv, carry[c]))
            return tuple(new)

        os = _causal_sweep(step, qi, 2, t)
        a = os[0] - lam * os[1]
        o_ref[:, g * LANES:(g + 1) * LANES] = _rms(a) * sg_ref[...] * (1.0 - lam_init)


def _dif_flash(qn, kn, v, slopes, lam_p, sub_g, lam_init, b, t):
    tq = min(DIF_FLASH_T, t)
    nq = t // tq
    return pl.pallas_call(
        functools.partial(_dif_flash_kernel, lam_init),
        grid=(b, DIF_KV_HEADS, nq),
        in_specs=[pl.BlockSpec((tq, 2 * LANES), lambda i, p, j: (i * nq + j, p)),
                  pl.BlockSpec((1, 1, t, DIF_VD), lambda i, p, j: (i, p, 0, 0)),
                  pl.BlockSpec((1, 1, t, DIF_VD), lambda i, p, j: (i, p, 0, 0)),
                  pl.BlockSpec(memory_space=pltpu.SMEM),
                  pl.BlockSpec(lam_p.shape, lambda i, p, j: (0, 0)),
                  pl.BlockSpec(sub_g.shape, lambda i, p, j: (0, 0))],
        out_specs=pl.BlockSpec((tq, 2 * LANES), lambda i, p, j: (i * nq + j, p)),
        out_shape=jax.ShapeDtypeStruct((b * t, SELF_W), F32),
        compiler_params=_params(("parallel", "parallel", "arbitrary")),
        name="dif_flash",
    )(qn, kn, v, slopes, lam_p, sub_g)


def _sample_cross(cq8, mkt, mvt):
    row = _iota((SUBLANES, MEM_W), 0)
    head = jnp.right_shift(_iota((SUBLANES, MEM_W), 1), 6)
    upper = (row >= 4).astype(jnp.int32)
    cqd = jnp.where(row < 4, cq8, pltpu.roll(cq8, 4, 0))
    q16 = jnp.concatenate([jnp.where(head == upper, cqd, 0.0),
                           jnp.where(head == 2 + upper, cqd, 0.0)], axis=0)
    s = _dot(q16, mkt)
    p = jnp.exp(s - jnp.max(s, axis=-1, keepdims=True))
    o = _dot_nt(p, mvt) / jnp.sum(p, axis=-1, keepdims=True)
    o0 = o[0:SUBLANES]
    o1 = o[SUBLANES:2 * SUBLANES]
    t0 = jnp.where(head == 0, o0, pltpu.roll(o0, 4, 0))
    t1 = jnp.where(head == 2, o1, pltpu.roll(o1, 4, 0))
    return jnp.where(head < 2, t0, t1)


def _softmax_update(s, m_sc, l_sc, acc_sc, pv):
    m_prev = m_sc[...]
    m_new = jnp.maximum(m_prev, jnp.max(s, axis=-1, keepdims=True))
    alpha = jnp.exp(m_prev - m_new)
    p = jnp.exp(s - m_new[:, 0:1])
    l_sc[...] = alpha * l_sc[...] + jnp.sum(p, axis=-1, keepdims=True)
    acc_sc[...] = alpha * acc_sc[...] + pv(p)
    m_sc[...] = m_new


def _mla_sample_kernel(pps, pt_ref, *refs):
    pages = refs[:pps]
    lhsq_ref, wt_ref, new_ref, wuv_ref, cq_ref, mk_ref, mv_ref = refs[pps:pps + 7]
    self_ref, cross_ref = refs[pps + 7:pps + 9]
    lhs_sc, m_sc, l_sc, acc_sc, new_sc = refs[pps + 9:]
    del pt_ref
    c = pl.program_id(1)
    nrow = MLA_HEADS * 4
    nk = MLA_HEADS * MLA_NOPE

    @pl.when(c == 0)
    def _():
        lhs_sc[0:nk, :] = wt_ref[...]
        lhs_sc[nk:nk + 2 * nrow, :] = lhsq_ref[0].astype(BF16)
        m_sc[...] = jnp.full(m_sc.shape, NEG_INF, F32)
        l_sc[...] = jnp.zeros(l_sc.shape, F32)
        acc_sc[...] = jnp.zeros(acc_sc.shape, F32)
        new_sc[...] = jnp.zeros(new_sc.shape, BF16)
        new_sc[0:MLA_ROW, :] = new_ref[0].astype(BF16)
        cross_ref[0] = _sample_cross(cq_ref[0], mk_ref[0, 0], mv_ref[0, 0])

    def process(tb, is_new):
        w = tb.shape[1]
        r_all = _dot(lhs_sc[...], tb)
        rs = []
        for hd in range(MLA_HEADS):
            kh = r_all[hd * MLA_NOPE:(hd + 1) * MLA_NOPE, :]
            ss = jnp.sum(kh * kh, axis=0, keepdims=True)
            rs.append(lax.rsqrt(ss * (1.0 / MLA_NOPE) + EPS))
        row8 = _iota((SUBLANES, w), 0)
        r48 = jnp.concatenate([jnp.where(row8 < 4, rs[2 * i], rs[2 * i + 1])
                               for i in range(MLA_HEADS // 2)], axis=0)
        s = r_all[nk:nk + nrow, :] * r48 + r_all[nk + nrow:nk + 2 * nrow, :]
        if is_new:
            qpos = _iota((nrow, w), 0) & 3
            s = jnp.where(_iota((nrow, w), 1) <= qpos, s, NEG_INF)
        _softmax_update(s, m_sc, l_sc, acc_sc, lambda p: _dot_nt(p, tb[0:KV_LORA, :]))

    wide = jnp.concatenate([pages[i][0, 0].astype(BF16) for i in range(pps)], axis=1)
    process(jnp.concatenate([wide, jnp.zeros((ROW_PAD - MLA_ROW, pps * PAGE), BF16)], axis=0), False)

    @pl.when(c == pl.num_programs(1) - 1)
    def _():
        process(new_sc[...], True)
        o_lat = acc_sc[...] / l_sc[...]
        lane = _iota((SUBLANES, LANES), 1)
        for i in range(MLA_HEADS // 2):
            res = _dot(o_lat[i * SUBLANES:(i + 1) * SUBLANES], wuv_ref[:, i * LANES:(i + 1) * LANES])
            self_ref[0, :, i * LANES:(i + 1) * LANES] = jnp.where(lane < MLA_VD, res, pltpu.roll(res, 4, 0))


def _page_spec(page_shape, layer_j, n_pages, pps, i):
    def index(s, c, pt):
        return (pt[s * n_pages + c * pps + i], layer_j) + (0,) * len(page_shape)
    return pl.BlockSpec((1, 1) + page_shape, index)


def _mla_sample(page_table, cache, layer_j, lhsq, w_t, rows_new, w_uv, cq8, memk, memv, layer_i):
    db, n_pages = page_table.shape
    pps = min(PAGES_PER_STEP, n_pages)
    nc = n_pages // pps
    nrow = MLA_HEADS * 4
    seq = lambda shape: pl.BlockSpec((1,) + shape, lambda s, c, pt: (s,) + (0,) * len(shape))
    const = lambda shape: pl.BlockSpec(shape, lambda s, c, pt: (0,) * len(shape))
    mem = pl.BlockSpec((1, 1, MEM_LEN, MEM_W), lambda s, c, pt: (s, layer_i, 0, 0))
    grid_spec = pltpu.PrefetchScalarGridSpec(
        num_scalar_prefetch=1,
        grid=(db, nc),
        in_specs=[_page_spec((MLA_ROW, PAGE), layer_j, n_pages, pps, i) for i in range(pps)] + [
            seq((2 * nrow, ROW_PAD)), const(w_t.shape), seq((MLA_ROW, LANES)), const(w_uv.shape),
            seq((SUBLANES, MEM_W)), mem, mem],
        out_specs=[seq((SUBLANES, SELF_W)), seq((SUBLANES, MEM_W))],
        scratch_shapes=[pltpu.VMEM((MLA_HEADS * MLA_NOPE + 2 * nrow, ROW_PAD), BF16),
                        pltpu.VMEM((nrow, LANES), F32), pltpu.VMEM((nrow, LANES), F32),
                        pltpu.VMEM((nrow, LANES), F32), pltpu.VMEM((ROW_PAD, LANES), BF16)])
    return pl.pallas_call(
        functools.partial(_mla_sample_kernel, pps),
        grid_spec=grid_spec,
        out_shape=[jax.ShapeDtypeStruct((db, SUBLANES, SELF_W), F32),
                   jax.ShapeDtypeStruct((db, SUBLANES, MEM_W), F32)],
        compiler_params=_params(("arbitrary", "arbitrary")),
        name="mla_sample",
    )(page_table.reshape(-1), *([cache] * pps), lhsq, w_t, rows_new, w_uv, cq8, memk, memv)


def _dif_sample_kernel(pps, lam_init, past, pt_ref, *refs):
    kpages = refs[:pps]
    vpages = refs[pps:2 * pps]
    (lhsq_ref, knew_ref, vnew_ref, nsl_ref, lam_ref, sg_ref,
     cq_ref, mk_ref, mv_ref) = refs[2 * pps:2 * pps + 9]
    self_ref, cross_ref = refs[2 * pps + 9:2 * pps + 11]
    m_sc, l_sc, acc_sc, knew_sc, vnew_sc = refs[2 * pps + 11:]
    del pt_ref
    c = pl.program_id(1)
    nrow = DIF_KV_HEADS * 16

    @pl.when(c == 0)
    def _():
        m_sc[...] = jnp.full(m_sc.shape, NEG_INF, F32)
        l_sc[...] = jnp.zeros(l_sc.shape, F32)
        acc_sc[...] = jnp.zeros(acc_sc.shape, F32)
        knew_sc[...] = jnp.zeros(knew_sc.shape, F32)
        vnew_sc[...] = jnp.zeros(vnew_sc.shape, F32)
        knew_sc[0:SUBLANES, :] = knew_ref[0]
        vnew_sc[0:SUBLANES, :] = vnew_ref[0]
        cross_ref[0] = _sample_cross(cq_ref[0], mk_ref[0, 0], mv_ref[0, 0])

    def process(kts, vts, base, is_new):
        q = lhsq_ref[0]
        s = jnp.concatenate(
            [jnp.concatenate([_dot_nt(q[kv * 16:(kv + 1) * 16], kt) for kt in kts[kv]], axis=1)
             for kv in range(DIF_KV_HEADS)], axis=0)
        w = s.shape[1]
        qrow = _iota((nrow, w), 0) & 3
        col = _iota((nrow, w), 1)
        dist = (past + qrow - base - col).astype(F32)
        s = s + nsl_ref[:, 0:1] * dist
        if is_new:
            s = jnp.where(col <= qrow, s, NEG_INF)

        def pv(p):
            outs = []
            for kv in range(DIF_KV_HEADS):
                o = 0
                for ti, vt in enumerate(vts[kv]):
                    rows = vt.shape[0]
                    o = o + _dot(p[kv * 16:(kv + 1) * 16, ti * rows:(ti + 1) * rows], vt)
                outs.append(o)
            return jnp.concatenate(outs, axis=0)
        _softmax_update(s, m_sc, l_sc, acc_sc, pv)

    def tiles(pg):
        return [[jnp.concatenate([pg[2 * i][0, 0, kv], pg[2 * i + 1][0, 0, kv]], axis=0)
                 for i in range(pps // 2)] for kv in range(DIF_KV_HEADS)]

    process(tiles(kpages), tiles(vpages), c * pps * PAGE, False)

    @pl.when(c == pl.num_programs(1) - 1)
    def _():
        heads = lambda ref: [[ref[:, kv * LANES:(kv + 1) * LANES]] for kv in range(DIF_KV_HEADS)]
        process(heads(knew_sc), heads(vnew_sc), past, True)
        a = acc_sc[...] / l_sc[...]
        lam = _dif_lambda(lam_ref[...], lam_init)
        for i in range(DIF_HEADS):
            an = a[i * SUBLANES:(i + 1) * SUBLANES]
            o = an - lam * pltpu.roll(an, 4, 0)
            self_ref[0, :, i * LANES:(i + 1) * LANES] = _rms(o) * sg_ref[...] * (1.0 - lam_init)


def _dif_sample(page_table, cache_k, cache_v, layer_j, lhsq, k_new, v_new, nsl, lam_p, sub_g,
                lam_init, cq8, memk, memv, layer_i):
    db, n_pages = page_table.shape
    pps = min(PAGES_PER_STEP, n_pages)
    nc = n_pages // pps
    nrow = DIF_KV_HEADS * 16
    seq = lambda shape: pl.BlockSpec((1,) + shape, lambda s, c, pt: (s,) + (0,) * len(shape))
    const = lambda shape: pl.BlockSpec(shape, lambda s, c, pt: (0,) * len(shape))
    mem = pl.BlockSpec((1, 1, MEM_LEN, MEM_W), lambda s, c, pt: (s, layer_i, 0, 0))
    pages = [_page_spec((DIF_KV_HEADS, PAGE, DIF_VD), layer_j, n_pages, pps, i) for i in range(pps)]
    grid_spec = pltpu.PrefetchScalarGridSpec(
        num_scalar_prefetch=1,
        grid=(db, nc),
        in_specs=pages + pages + [
            seq((nrow, LANES)), seq((SUBLANES, DIF_KW)), seq((SUBLANES, DIF_KW)), const(nsl.shape),
            const(lam_p.shape), const(sub_g.shape), seq((SUBLANES, MEM_W)), mem, mem],
        out_specs=[seq((SUBLANES, SELF_W)), seq((SUBLANES, MEM_W))],
        scratch_shapes=[pltpu.VMEM((nrow, LANES), F32), pltpu.VMEM((nrow, LANES), F32),
                        pltpu.VMEM((nrow, LANES), F32), pltpu.VMEM((PAGE, DIF_KW), F32),
                        pltpu.VMEM((PAGE, DIF_KW), F32)])
    return pl.pallas_call(
        functools.partial(_dif_sample_kernel, pps, lam_init, n_pages * PAGE),
        grid_spec=grid_spec,
        out_shape=[jax.ShapeDtypeStruct((db, SUBLANES, SELF_W), F32),
                   jax.ShapeDtypeStruct((db, SUBLANES, MEM_W), F32)],
        compiler_params=_params(("arbitrary", "arbitrary")),
        name="dif_sample",
    )(page_table.reshape(-1), *([cache_k] * pps), *([cache_v] * pps), lhsq, k_new, v_new, nsl,
      lam_p, sub_g, cq8, memk, memv)


def _out_kernel(with_cross, x_ref, self_ref, c_ref, gs_ref, *rest):
    if with_cross:
        mk_ref, mv_ref, w_ref, y_ref = rest
        tm = x_ref.shape[0]
        cq = c_ref[...]
        mk = mk_ref[0, 0].astype(BF16)
        mv = mv_ref[0, 0].astype(BF16)
        head = jnp.right_shift(_iota((tm, MEM_W), 1), 6)
        cross = jnp.zeros((tm, MEM_W), F32)
        for hd in range(MEM_HEADS):
            s = _dot_nt(jnp.where(head == hd, cq, 0.0), mk)
            p = jnp.exp(s - jnp.max(s, axis=-1, keepdims=True))
            o = _dot(p, mv) / jnp.sum(p, axis=-1, keepdims=True)
            cross = jnp.where(head == hd, o, cross)
    else:
        w_ref, y_ref = rest
        cross = c_ref[...]
    mix = jnp.concatenate([self_ref[...], cross], axis=-1) * gs_ref[...]
    y_ref[...] = x_ref[...] + _dot(mix, w_ref[...])


def _out_proj(x, self_o, c, gs, w, memk=None, memv=None, layer_i=0, tokens_per_batch=None):
    n = x.shape[0]
    tm = min(PROJ_TM, n)
    tok = lambda wd: pl.BlockSpec((tm, wd), lambda i: (i, 0))
    ins = [x, self_o, c, gs]
    specs = [tok(D_MODEL), tok(SELF_W), tok(MEM_W), tok(MIX_W)]
    with_cross = memk is not None
    if with_cross:
        per = tokens_per_batch // tm
        mem = pl.BlockSpec((1, 1, MEM_LEN, MEM_W), lambda i: (i // per, layer_i, 0, 0))
        ins += [memk, memv]
        specs += [mem, mem]
    ins.append(w)
    specs.append(_full(w.shape))
    return pl.pallas_call(
        functools.partial(_out_kernel, with_cross),
        grid=(n // tm,),
        in_specs=specs,
        out_specs=tok(D_MODEL),
        out_shape=jax.ShapeDtypeStruct((n, D_MODEL), F32),
        compiler_params=_params(("parallel",)),
        name="out_proj_prompt" if with_cross else "out_proj_sample",
    )(*ins)


def _rope_tables(pos, g_nope, g_rope, scale):
    half = MLA_ROPE // 2
    inv_freq = ROPE_THETA ** (-jnp.arange(half, dtype=F32) / half)
    ang = pos.astype(F32)[:, None] * inv_freq
    c, s = jnp.cos(ang), jnp.sin(ang)
    g1, g2 = g_rope[:half].astype(F32), g_rope[half:].astype(F32)
    n = pos.shape[0]
    z32 = jnp.zeros((n, LANES - MLA_NOPE - MLA_ROPE), F32)
    a = jnp.concatenate([jnp.broadcast_to(g_nope.astype(F32), (n, MLA_NOPE)), g1 * c, g2 * c, z32], axis=1)
    b = jnp.concatenate([jnp.zeros((n, MLA_NOPE), F32), -g2 * s, g1 * s, z32], axis=1)
    return a * scale, b * scale


def _row(v):
    return v.astype(F32).reshape(1, -1)


def _mla_weights(i, j, norm_g, cq_norm_g, mla_w_in, mla_q_a_norm_g, mla_w_q_up, mla_kv_a_norm_g,
                 mla_w_uk, mla_w_uv, mla_kn_g):
    w = mla_w_in[j]
    c0 = Q_LORA + KV_LORA
    x1 = w[:, c0:c0 + 16]
    x2 = w[:, c0 + 16:c0 + 32]
    w_in = jnp.concatenate([w[:, :c0], jnp.zeros((D_MODEL, 64), w.dtype), x1, x2, x2, x1,
                            w[:, c0 + 32:]], axis=1).astype(BF16)
    wq = mla_w_q_up[j].reshape(Q_LORA, MLA_HEADS, MLA_NOPE + MLA_ROPE)
    qn, q1, q2 = wq[..., :MLA_NOPE], wq[..., MLA_NOPE:MLA_NOPE + 16], wq[..., MLA_NOPE + 16:]
    w_q = jnp.concatenate([qn, q1, q2, q2, q1], axis=-1).reshape(Q_LORA, MLA_HEADS * LANES).astype(BF16)
    wuk = mla_w_uk[j].reshape(KV_LORA, MLA_HEADS, MLA_NOPE)
    w_uk_pad = jnp.pad(wuk, ((0, 0), (0, 0), (0, LANES - MLA_NOPE))).reshape(KV_LORA, -1).astype(BF16)
    w_t = jnp.pad(mla_w_uk[j].T, ((0, 0), (0, ROW_PAD - KV_LORA))).astype(BF16)
    wabs = jnp.transpose(wuk * mla_kn_g[j].astype(F32), (1, 2, 0))
    w_abs = jnp.pad(wabs, ((0, 0), (0, LANES - MLA_NOPE), (0, 0))).astype(BF16)
    kn_g = _row(jnp.pad(mla_kn_g[j], (0, LANES - MLA_NOPE)))
    return dict(g=_row(norm_g[i]), w_in=w_in, q_a_g=_row(mla_q_a_norm_g[j]), w_q=w_q,
                kv_a_g=_row(mla_kv_a_norm_g[j]), w_uk_pad=w_uk_pad, w_t=w_t, w_abs=w_abs, kn_g=kn_g,
                w_uv=mla_w_uv[j].astype(BF16),
                cq_g=_row(jnp.tile(cq_norm_g[i], 2)) * MEM_HD ** -0.5)


def kernel(x_prompt, x_sample, cache_mla_kv, cache_diff_k, cache_diff_v, cache_mem_k, cache_mem_v,
           page_table, mem_prompt, norm_g, w_out, mem_norm_g, w_mem_kv, cq_norm_g, ck_norm_g,
           mla_w_in, mla_q_a_norm_g, mla_w_q_up, mla_kv_a_norm_g, mla_w_uk, mla_w_uv,
           mla_qn_g, mla_kn_g, mla_qr_g, mla_kr_g,
           dif_w_in, dif_q_g, dif_k_g, dif_lambda, dif_subln_g):
    b, t, d = x_prompt.shape
    db, ds, _ = x_sample.shape
    n_pages = page_table.shape[1]
    past = n_pages * PAGE
    depth = norm_g.shape[0]
    assert d == D_MODEL and ds == 4 and t % max(PROJ_TM, MLA_FLASH_T, DIF_FLASH_T) == 0
    assert n_pages % min(PAGES_PER_STEP, n_pages) == 0 and min(PAGES_PER_STEP, n_pages) % 2 == 0

    pos_p = jnp.arange(t, dtype=jnp.int32)
    pos_s = jnp.tile(past + jnp.arange(ds, dtype=jnp.int32), db)
    xp = x_prompt.reshape(b * t, d)
    xs = x_sample.reshape(db * ds, d)
    w_out_b = w_out.astype(BF16)

    memk, memv = _mem_kv(mem_prompt, mem_norm_g.astype(F32).reshape(depth, 1, d), w_mem_kv.astype(BF16),
                         jnp.tile(ck_norm_g.astype(F32), (1, 2)).reshape(depth, 1, LANES))
    smem_k = jnp.transpose(cache_mem_k, (0, 1, 3, 4, 2)).reshape(db, depth, MEM_W, MEM_LEN)
    smem_v = jnp.transpose(cache_mem_v, (0, 1, 3, 4, 2)).reshape(db, depth, MEM_W, MEM_LEN)
    cache_mla_t = jnp.swapaxes(cache_mla_kv, 2, 3)
    cache_dk = jnp.transpose(cache_diff_k, (0, 1, 3, 2, 4))
    cache_dv = jnp.transpose(cache_diff_v, (0, 1, 3, 2, 4))

    def pad8(a, k=1):
        return jnp.pad(a.reshape(db, ds, -1), ((0, 0), (0, k * SUBLANES - ds), (0, 0)))

    mla_rows_p, mla_rows_s, dk_p, dv_p, dk_s, dv_s = [], [], [], [], [], []
    slopes = 2.0 ** (-8.0 * jnp.arange(1, DIF_HEADS + 1, dtype=F32) / DIF_HEADS)

    for i in range(depth):
        j = i // 2
        if i % 2 == 0:
            lw = _mla_weights(i, j, norm_g, cq_norm_g, mla_w_in, mla_q_a_norm_g, mla_w_q_up,
                              mla_kv_a_norm_g, mla_w_uk, mla_w_uv, mla_kn_g)
            scale = (MLA_NOPE + MLA_ROPE) ** -0.5
            tabs_p = (*_rope_tables(pos_p, mla_qn_g[j], mla_qr_g[j], scale),
                      *_rope_tables(pos_p, jnp.zeros((MLA_NOPE,), F32), mla_kr_g[j], 1.0))
            tabs_s = (*_rope_tables(pos_s, mla_qn_g[j], mla_qr_g[j], scale),
                      *_rope_tables(pos_s, jnp.zeros((MLA_NOPE,), F32), mla_kr_g[j], 1.0))
            qhat, khat, v_p, rows_p, cq_p, gs_p = _mla_proj(xp, tabs_p, lw, False)
            qabs, qrope, rows_s, cq_s, gs_s = _mla_proj(xs, tabs_s, lw, True)
            self_p = _mla_flash(qhat, khat, v_p, b, t)

            def to_rows(a):
                return a.reshape(db, ds, MLA_HEADS, LANES).transpose(0, 2, 1, 3).reshape(db, MLA_HEADS * ds, LANES)
            lhs_n = jnp.pad(to_rows(qabs), ((0, 0), (0, 0), (0, ROW_PAD - KV_LORA)))
            lhs_r = jnp.pad(to_rows(qrope), ((0, 0), (0, 0), (KV_LORA, 0)))
            lhsq = jnp.concatenate([lhs_n, lhs_r], axis=1)
            new_t = jnp.pad(jnp.swapaxes(rows_s.reshape(db, ds, MLA_ROW), 1, 2),
                            ((0, 0), (0, 0), (0, LANES - ds)))
            self_s, cross_s = _mla_sample(page_table, cache_mla_t, j, lhsq, lw["w_t"], new_t,
                                          lw["w_uv"], pad8(cq_s), smem_k, smem_v, i)
            mla_rows_p.append(rows_p.reshape(b, t, MLA_ROW))
            mla_rows_s.append(rows_s.reshape(db, ds, MLA_ROW))
        else:
            lam_init = 0.8 - 0.6 * math.exp(-0.3 * i)
            lw = dict(g=_row(norm_g[i]), w_in=dif_w_in[j].astype(BF16),
                      q_g=_row(jnp.tile(dif_q_g[j], 2)) * DIF_HD ** -0.5, k_g=_row(jnp.tile(dif_k_g[j], 2)),
                      cq_g=_row(jnp.tile(cq_norm_g[i], 2)) * MEM_HD ** -0.5)
            lam_p = dif_lambda[j].astype(F32)
            sub_g = _row(dif_subln_g[j])
            qn_p, kn_p, v_p, cq_p, gs_p = _dif_proj(xp, lw, b)
            qn_s, kn_s, v_s, cq_s, gs_s = _dif_proj(xs, lw, 1)
            to_tok = lambda a: jnp.transpose(a[0], (1, 0, 2)).reshape(db * ds, DIF_KW)
            kn_s, v_s = to_tok(kn_s), to_tok(v_s)
            slope_tab = slopes.reshape(DIF_KV_HEADS, 2)
            self_p = _dif_flash(qn_p, kn_p, v_p, slope_tab, lam_p, sub_g, lam_init, b, t)

            q5 = qn_s.reshape(db, ds, DIF_KV_HEADS, 2, 1, LANES).transpose(0, 2, 3, 4, 1, 5)
            lane_map = (jnp.arange(LANES) // DIF_HD)[None, None, None, None, None, :]
            q6 = jnp.where(lane_map == jnp.arange(2)[None, None, None, :, None, None], q5, 0.0)
            lhsq = q6.reshape(db, DIF_KV_HEADS * 16, LANES)
            nsl = jnp.broadcast_to(-jnp.repeat(slopes, 8)[:, None], (DIF_KV_HEADS * 16, LANES))
            self_s, cross_s = _dif_sample(page_table, cache_dk, cache_dv, j, lhsq, pad8(kn_s), pad8(v_s),
                                          nsl, lam_p, sub_g, lam_init, pad8(cq_s), smem_k, smem_v, i)
            dk_p.append(kn_p)
            dv_p.append(v_p)
            dk_s.append(kn_s.reshape(db, ds, DIF_KV_HEADS, DIF_VD))
            dv_s.append(v_s.reshape(db, ds, DIF_KV_HEADS, DIF_VD))

        xp = _out_proj(xp, self_p, cq_p, gs_p, w_out_b[i], memk, memv, i, t)
        xs = _out_proj(xs, self_s[:, :ds].reshape(db * ds, SELF_W), cross_s[:, :ds].reshape(db * ds, MEM_W),
                       gs_s, w_out_b[i])

    mem_shape = (b, depth, MEM_LEN, MEM_HEADS, MEM_HD)
    tok_major = lambda parts: jnp.transpose(jnp.stack(parts, axis=1), (0, 1, 3, 2, 4))
    return (xp.reshape(b, t, d), xs.reshape(db, ds, d),
            jnp.stack(mla_rows_p, axis=1), tok_major(dk_p), tok_major(dv_p),
            memk.reshape(mem_shape), memv.reshape(mem_shape),
            jnp.stack(mla_rows_s, axis=1), jnp.stack(dk_s, axis=1), jnp.stack(dv_s, axis=1))
```

```python
import functools
import math

import jax
import jax.numpy as jnp
from jax import lax
from jax.experimental import pallas as pl
from jax.experimental.pallas import tpu as pltpu

F32 = jnp.float32
BF16 = jnp.bfloat16

LANES = 128
SUBLANES = 8
VMEM_LIMIT = 56 * 1024 * 1024

D_MODEL = 1024
PAGE = 128
MEM_LEN = 256
MEM_HEADS = 4
MEM_HD = 64
MEM_W = MEM_HEADS * MEM_HD
MIX_W = D_MODEL
SELF_W = MIX_W - MEM_W
MLA_HEADS = 12
MLA_NOPE = 64
MLA_ROPE = 32
MLA_VD = 64
Q_LORA = 256
KV_LORA = 128
MLA_ROW = KV_LORA + MLA_ROPE
ROW_PAD = 2 * LANES
ROPE_THETA = 10000.0
DIF_HEADS = 6
DIF_KV_HEADS = 3
DIF_HD = 64
DIF_VD = 2 * DIF_HD
DIF_QW = DIF_HEADS * DIF_VD
DIF_KW = DIF_KV_HEADS * DIF_VD
EPS = 1e-6
NEG_INF = float("-inf")

PROJ_TM = 512
MLA_FLASH_T = 512
DIF_FLASH_T = 512
PAGES_PER_STEP = 32


def _dot(a, b):
    return jnp.dot(a.astype(BF16), b.astype(BF16), preferred_element_type=F32)


def _dot_nt(a, b):
    return lax.dot_general(a.astype(BF16), b.astype(BF16), (((1,), (1,)), ((), ())),
                           preferred_element_type=F32)


def _iota(shape, dim):
    return lax.broadcasted_iota(jnp.int32, shape, dim)


def _rms(x):
    return x * lax.rsqrt(jnp.mean(x * x, axis=-1, keepdims=True) + EPS)


def _seg_rsqrt(sq, lane, lo, hi):
    ss = jnp.sum(jnp.where((lane >= lo) & (lane < hi), sq, 0.0), axis=-1, keepdims=True)
    return lax.rsqrt(ss * (1.0 / (hi - lo)) + EPS)


def _norm_halves(blk, lane):
    sq = blk * blk
    r0 = _seg_rsqrt(sq, lane, 0, 64)
    r1 = _seg_rsqrt(sq, lane, 64, 128)
    return blk * jnp.where(lane < 64, r0, r1)


def _silu(x):
    return x / (1.0 + jnp.exp(-x))


def _params(sem):
    return pltpu.CompilerParams(dimension_semantics=sem, vmem_limit_bytes=VMEM_LIMIT)


def _cross_query_and_gate(z_cq, z_gate, cqg, cqn_ref, gs_ref, lane):
    for p in range(MEM_W // LANES):
        blk = z_cq[:, p * LANES:(p + 1) * LANES]
        cqn_ref[:, p * LANES:(p + 1) * LANES] = _norm_halves(blk, lane) * cqg
    gs_ref[...] = _silu(z_gate)


def _mla_proj_kernel(sample, x_ref, g_ref, win_ref, qag_ref, wq_ref, kvg_ref,
                     qa_ref, qb_ref, ka_ref, kb_ref, cqg_ref, *rest):
    if sample:
        wabs_ref, qabs_ref, qrope_ref, rows_ref, cqn_ref, gs_ref = rest
    else:
        wuk_ref, kng_ref, wuv_ref, qhat_ref, khat_ref, v_ref, rows_ref, cqn_ref, gs_ref = rest
    tm = x_ref.shape[0]
    lane = _iota((tm, LANES), 1)

    h = _rms(x_ref[...]) * g_ref[...]
    z = _dot(h, win_ref[...])
    z_q = z[:, 0:256]
    z_c = z[:, 256:384]
    z_r = z[:, 384:512]
    _cross_query_and_gate(z[:, 512:768], z[:, 768:1792], cqg_ref[...], cqn_ref, gs_ref, lane)

    q = _dot(_rms(z_q) * qag_ref[...], wq_ref[...])
    qa = qa_ref[...]
    qb = qb_ref[...]
    for hd in range(MLA_HEADS):
        blk = q[:, hd * LANES:(hd + 1) * LANES]
        sq = blk * blk
        r = jnp.where(lane < 64, _seg_rsqrt(sq, lane, 0, 64), _seg_rsqrt(sq, lane, 64, 96))
        qh = (blk * qa + pltpu.roll(blk, 96, 1) * qb) * r
        if sample:
            qabs_ref[:, hd * LANES:(hd + 1) * LANES] = _dot(qh, wabs_ref[hd])
            qrope_ref[:, hd * LANES:(hd + 1) * LANES] = jnp.where(
                lane < MLA_ROPE, pltpu.roll(qh, 64, 1), 0.0)
        else:
            qhat_ref[:, hd * LANES:(hd + 1) * LANES] = qh

    lat = _rms(z_c) * kvg_ref[...]
    rk = _seg_rsqrt(z_r * z_r, lane, 64, 96)
    kr = (z_r * ka_ref[...] + pltpu.roll(z_r, 96, 1) * kb_ref[...]) * rk
    rows_ref[:, 0:KV_LORA] = lat
    rows_ref[:, KV_LORA:MLA_ROW] = pltpu.roll(kr, 64, 1)[:, 0:MLA_ROPE]

    if not sample:
        kx = _dot(lat, wuk_ref[...])
        kng = kng_ref[...]
        for hd in range(MLA_HEADS):
            blk = kx[:, hd * LANES:(hd + 1) * LANES]
            r = _seg_rsqrt(blk * blk, lane, 0, 64)
            khat_ref[:, hd * LANES:(hd + 1) * LANES] = blk * r * kng + kr
        v_ref[...] = _dot(lat, wuv_ref[...])


def _dif_proj_kernel(x_ref, g_ref, win_ref, qg_ref, kg_ref, cqg_ref,
                     qn_ref, kn_ref, v_ref, cqn_ref, gs_ref):
    tm = x_ref.shape[0]
    lane = _iota((tm, LANES), 1)
    h = _rms(x_ref[...]) * g_ref[...]
    z = _dot(h, win_ref[...])
    o = DIF_QW + 2 * DIF_KW
    _cross_query_and_gate(z[:, o:o + MEM_W], z[:, o + MEM_W:], cqg_ref[...], cqn_ref, gs_ref, lane)
    qg = qg_ref[...]
    for hd in range(DIF_HEADS):
        qn_ref[:, hd * LANES:(hd + 1) * LANES] = _norm_halves(z[:, hd * LANES:(hd + 1) * LANES], lane) * qg
    kg = kg_ref[...]
    for hd in range(DIF_KV_HEADS):
        blk = z[:, DIF_QW + hd * LANES:DIF_QW + (hd + 1) * LANES]
        kn_ref[0, hd] = _norm_halves(blk, lane) * kg
        v_ref[0, hd] = z[:, DIF_QW + DIF_KW + hd * LANES:DIF_QW + DIF_KW + (hd + 1) * LANES]


def _full(shape):
    nd = len(shape)
    return pl.BlockSpec(shape, lambda i: (0,) * nd)


def _mla_proj(x, pos_tabs, lw, sample):
    n = x.shape[0]
    tm = min(PROJ_TM, n)
    npos = pos_tabs[0].shape[0] // tm
    tok = lambda w: pl.BlockSpec((tm, w), lambda i: (i, 0))
    pos = pl.BlockSpec((tm, LANES), lambda i: (i % npos, 0))
    ins = [x, lw["g"], lw["w_in"], lw["q_a_g"], lw["w_q"], lw["kv_a_g"], *pos_tabs, lw["cq_g"]]
    specs = [tok(D_MODEL), _full(lw["g"].shape), _full(lw["w_in"].shape), _full(lw["q_a_g"].shape),
             _full(lw["w_q"].shape), _full(lw["kv_a_g"].shape), pos, pos, pos, pos,
             _full(lw["cq_g"].shape)]
    hw = MLA_HEADS * LANES
    if sample:
        ins += [lw["w_abs"]]
        specs += [_full(lw["w_abs"].shape)]
        outs = [(hw, tok(hw)), (hw, tok(hw))]
    else:
        ins += [lw["w_uk_pad"], lw["kn_g"], lw["w_uv"]]
        specs += [_full(lw["w_uk_pad"].shape), _full(lw["kn_g"].shape), _full(lw["w_uv"].shape)]
        outs = [(hw, tok(hw)), (hw, tok(hw)), (SELF_W, tok(SELF_W))]
    outs += [(MLA_ROW, tok(MLA_ROW)), (MEM_W, tok(MEM_W)), (MIX_W, tok(MIX_W))]
    return pl.pallas_call(
        functools.partial(_mla_proj_kernel, sample),
        grid=(n // tm,),
        in_specs=specs,
        out_specs=[s for _, s in outs],
        out_shape=[jax.ShapeDtypeStruct((n, w), F32) for w, _ in outs],
        compiler_params=_params(("parallel",)),
        name="mla_proj_sample" if sample else "mla_proj_prompt",
    )(*ins)


def _dif_proj(x, lw, nb):
    n = x.shape[0]
    tpb = n // nb
    tm = min(PROJ_TM, tpb)
    per = tpb // tm
    tok = lambda w: pl.BlockSpec((tm, w), lambda i: (i, 0))
    kv = pl.BlockSpec((1, DIF_KV_HEADS, tm, DIF_VD), lambda i: (i // per, 0, i % per, 0))
    kv_shape = jax.ShapeDtypeStruct((nb, DIF_KV_HEADS, tpb, DIF_VD), F32)
    ins = [x, lw["g"], lw["w_in"], lw["q_g"], lw["k_g"], lw["cq_g"]]
    specs = [tok(D_MODEL)] + [_full(a.shape) for a in ins[1:]]
    flat = lambda w: jax.ShapeDtypeStruct((n, w), F32)
    return pl.pallas_call(
        _dif_proj_kernel,
        grid=(n // tm,),
        in_specs=specs,
        out_specs=[tok(DIF_QW), kv, kv, tok(MEM_W), tok(MIX_W)],
        out_shape=[flat(DIF_QW), kv_shape, kv_shape, flat(MEM_W), flat(MIX_W)],
        compiler_params=_params(("parallel",)),
        name="dif_proj",
    )(*ins)


def _mem_kv_kernel(mem_ref, g_ref, w_ref, ckg_ref, mk_ref, mv_ref):
    m = mem_ref.shape[1]
    lane = _iota((m, LANES), 1)
    kv = _dot(_rms(mem_ref[0]) * g_ref[0], w_ref[0])
    ckg = ckg_ref[0]
    for p in range(MEM_W // LANES):
        mk_ref[0, 0, :, p * LANES:(p + 1) * LANES] = _norm_halves(kv[:, p * LANES:(p + 1) * LANES], lane) * ckg
    mv_ref[0, 0] = kv[:, MEM_W:]


def _mem_kv(mem, g, w, ckg):
    b, m, _ = mem.shape
    depth = g.shape[0]
    out = pl.BlockSpec((1, 1, m, MEM_W), lambda l, i: (i, l, 0, 0))
    return pl.pallas_call(
        _mem_kv_kernel,
        grid=(depth, b),
        in_specs=[pl.BlockSpec((1, m, D_MODEL), lambda l, i: (i, 0, 0)),
                  pl.BlockSpec((1, 1, D_MODEL), lambda l, i: (l, 0, 0)),
                  pl.BlockSpec((1, D_MODEL, 2 * MEM_W), lambda l, i: (l, 0, 0)),
                  pl.BlockSpec((1, 1, LANES), lambda l, i: (l, 0, 0))],
        out_specs=[out, out],
        out_shape=[jax.ShapeDtypeStruct((b, depth, m, MEM_W), F32)] * 2,
        compiler_params=_params(("parallel", "parallel")),
        name="mem_kv",
    )(mem, g, w, ckg)


def _online_step(s, v, carry):
    m, l, acc = carry
    m_new = jnp.maximum(m, jnp.max(s, axis=-1, keepdims=True))
    alpha = jnp.exp(m - m_new)
    p = jnp.exp(s - m_new)
    l = alpha * l + jnp.sum(p, axis=-1, keepdims=True)
    acc = alpha * acc + _dot(p, v)
    return m_new, l, acc


def _flash_init(t):
    return (jnp.full((t, 1), NEG_INF, F32), jnp.zeros((t, 1), F32), jnp.zeros((t, LANES), F32))


def _causal_sweep(step, qi, n_chains, t):
    init = tuple(_flash_init(t) for _ in range(n_chains))
    carry = lax.fori_loop(0, qi, functools.partial(step, diag=False), init)
    return [acc / l for _, l, acc in step(qi, carry, True)]


def _mla_flash_kernel(q_ref, k_ref, v_ref, o_ref):
    t = q_ref.shape[0]
    qi = pl.program_id(2)
    lane = _iota((t, LANES), 1)
    qs = [q_ref[:, hh * LANES:(hh + 1) * LANES].astype(BF16) for hh in range(2)]

    def step(j, carry, diag):
        rows = pl.ds(pl.multiple_of(j * t, t), t)
        kk = k_ref[rows, :].astype(BF16)
        vv = v_ref[rows, :].astype(BF16)
        new = []
        for hh in range(2):
            s = _dot_nt(qs[hh], kk[:, hh * LANES:(hh + 1) * LANES])
            if diag:
                s = jnp.where(_iota((t, t), 1) <= _iota((t, t), 0), s, NEG_INF)
            new.append(_online_step(s, vv, carry[hh]))
        return tuple(new)

    outs = _causal_sweep(step, qi, 2, t)
    o_ref[...] = jnp.where(lane < MLA_VD, outs[0], outs[1])


def _mla_flash(qhat, khat, v, b, t):
    tq = min(MLA_FLASH_T, t)
    nq = t // tq
    return pl.pallas_call(
        _mla_flash_kernel,
        grid=(b, MLA_HEADS // 2, nq),
        in_specs=[pl.BlockSpec((tq, 2 * LANES), lambda i, p, j: (i * nq + j, p)),
                  pl.BlockSpec((t, 2 * LANES), lambda i, p, j: (i, p)),
                  pl.BlockSpec((t, LANES), lambda i, p, j: (i, p))],
        out_specs=pl.BlockSpec((tq, LANES), lambda i, p, j: (i * nq + j, p)),
        out_shape=jax.ShapeDtypeStruct((b * t, SELF_W), F32),
        compiler_params=_params(("parallel", "parallel", "arbitrary")),
        name="mla_flash",
    )(qhat, khat, v)


def _dif_lambda(lp, lam_init):
    a = jnp.sum(lp[0:1] * lp[1:2], axis=-1, keepdims=True)
    b = jnp.sum(lp[2:3] * lp[3:4], axis=-1, keepdims=True)
    return jnp.exp(a) - jnp.exp(b) + lam_init


def _dif_flash_kernel(lam_init, q_ref, k_ref, v_ref, slope_ref, lam_ref, sg_ref, o_ref):
    t = q_ref.shape[0]
    qi = pl.program_id(2)
    row = _iota((t, t), 0)
    col = _iota((t, t), 1)
    causal = col <= row
    lane = _iota((t, LANES), 1)
    lam = _dif_lambda(lam_ref[...], lam_init)
    kcol = _iota((1, t), 1)
    for g in range(2):
        slope = slope_ref[pl.program_id(1), g]
        qc = [jnp.where((lane >= c * DIF_HD) & (lane < (c + 1) * DIF_HD),
                        q_ref[:, g * LANES:(g + 1) * LANES], 0.0).astype(BF16) for c in range(2)]

        def step(j, carry, diag, slope=slope, qc=qc):
            rows = pl.ds(pl.multiple_of(j * t, t), t)
            kk = k_ref[0, 0, rows, :].astype(BF16)
            vv = v_ref[0, 0, rows, :].astype(BF16)
            nb = (kcol + (j - qi) * t).astype(F32) * slope
            new = []
            for c in range(2):
                s = _dot_nt(qc[c], kk) + nb
                if diag:
                    s = jnp.where(causal, s, NEG_INF)
                new.append(_online_step(s, vv, carry[c]))
            return tuple(new)

        os = _causal_sweep(step, qi, 2, t)
        a = os[0] - lam * os[1]
        o_ref[:, g * LANES:(g + 1) * LANES] = _rms(a) * sg_ref[...] * (1.0 - lam_init)


def _dif_flash(qn, kn, v, slopes, lam_p, sub_g, lam_init, b, t):
    tq = min(DIF_FLASH_T, t)
    nq = t // tq
    return pl.pallas_call(
        functools.partial(_dif_flash_kernel, lam_init),
        grid=(b, DIF_KV_HEADS, nq),
        in_specs=[pl.BlockSpec((tq, 2 * LANES), lambda i, p, j: (i * nq + j, p)),
                  pl.BlockSpec((1, 1, t, DIF_VD), lambda i, p, j: (i, p, 0, 0)),
                  pl.BlockSpec((1, 1, t, DIF_VD), lambda i, p, j: (i, p, 0, 0)),
                  pl.BlockSpec(memory_space=pltpu.SMEM),
                  pl.BlockSpec(lam_p.shape, lambda i, p, j: (0, 0)),
                  pl.BlockSpec(sub_g.shape, lambda i, p, j: (0, 0))],
        out_specs=pl.BlockSpec((tq, 2 * LANES), lambda i, p, j: (i * nq + j, p)),
        out_shape=jax.ShapeDtypeStruct((b * t, SELF_W), F32),
        compiler_params=_params(("parallel", "parallel", "arbitrary")),
        name="dif_flash",
    )(qn, kn, v, slopes, lam_p, sub_g)


def _sample_cross(cq8, mkt, mvt):
    row = _iota((SUBLANES, MEM_W), 0)
    head = jnp.right_shift(_iota((SUBLANES, MEM_W), 1), 6)
    upper = (row >= 4).astype(jnp.int32)
    cqd = jnp.where(row < 4, cq8, pltpu.roll(cq8, 4, 0))
    q16 = jnp.concatenate([jnp.where(head == upper, cqd, 0.0),
                           jnp.where(head == 2 + upper, cqd, 0.0)], axis=0)
    s = _dot(q16, mkt)
    p = jnp.exp(s - jnp.max(s, axis=-1, keepdims=True))
    o = _dot_nt(p, mvt) / jnp.sum(p, axis=-1, keepdims=True)
    o0 = o[0:SUBLANES]
    o1 = o[SUBLANES:2 * SUBLANES]
    t0 = jnp.where(head == 0, o0, pltpu.roll(o0, 4, 0))
    t1 = jnp.where(head == 2, o1, pltpu.roll(o1, 4, 0))
    return jnp.where(head < 2, t0, t1)


def _softmax_update(s, m_sc, l_sc, acc_sc, pv):
    m_prev = m_sc[...]
    m_new = jnp.maximum(m_prev, jnp.max(s, axis=-1, keepdims=True))
    alpha = jnp.exp(m_prev - m_new)
    p = jnp.exp(s - m_new[:, 0:1])
    l_sc[...] = alpha * l_sc[...] + jnp.sum(p, axis=-1, keepdims=True)
    acc_sc[...] = alpha * acc_sc[...] + pv(p)
    m_sc[...] = m_new


def _mla_sample_kernel(pps, pt_ref, *refs):
    pages = refs[:pps]
    lhsq_ref, wt_ref, new_ref, wuv_ref, cq_ref, mk_ref, mv_ref = refs[pps:pps + 7]
    self_ref, cross_ref = refs[pps + 7:pps + 9]
    lhs_sc, m_sc, l_sc, acc_sc, new_sc = refs[pps + 9:]
    del pt_ref
    c = pl.program_id(1)
    nrow = MLA_HEADS * 4
    nk = MLA_HEADS * MLA_NOPE

    @pl.when(c == 0)
    def _():
        lhs_sc[0:nk, :] = wt_ref[...]
        lhs_sc[nk:nk + 2 * nrow, :] = lhsq_ref[0].astype(BF16)
        m_sc[...] = jnp.full(m_sc.shape, NEG_INF, F32)
        l_sc[...] = jnp.zeros(l_sc.shape, F32)
        acc_sc[...] = jnp.zeros(acc_sc.shape, F32)
        new_sc[...] = jnp.zeros(new_sc.shape, BF16)
        new_sc[0:MLA_ROW, :] = new_ref[0].astype(BF16)
        cross_ref[0] = _sample_cross(cq_ref[0], mk_ref[0, 0], mv_ref[0, 0])

    def process(tb, is_new):
        w = tb.shape[1]
        r_all = _dot(lhs_sc[...], tb)
        rs = []
        for hd in range(MLA_HEADS):
            kh = r_all[hd * MLA_NOPE:(hd + 1) * MLA_NOPE, :]
            ss = jnp.sum(kh * kh, axis=0, keepdims=True)
            rs.append(lax.rsqrt(ss * (1.0 / MLA_NOPE) + EPS))
        row8 = _iota((SUBLANES, w), 0)
        r48 = jnp.concatenate([jnp.where(row8 < 4, rs[2 * i], rs[2 * i + 1])
                               for i in range(MLA_HEADS // 2)], axis=0)
        s = r_all[nk:nk + nrow, :] * r48 + r_all[nk + nrow:nk + 2 * nrow, :]
        if is_new:
            qpos = _iota((nrow, w), 0) & 3
            s = jnp.where(_iota((nrow, w), 1) <= qpos, s, NEG_INF)
        _softmax_update(s, m_sc, l_sc, acc_sc, lambda p: _dot_nt(p, tb[0:KV_LORA, :]))

    wide = jnp.concatenate([pages[i][0, 0].astype(BF16) for i in range(pps)], axis=1)
    process(jnp.concatenate([wide, jnp.zeros((ROW_PAD - MLA_ROW, pps * PAGE), BF16)], axis=0), False)

    @pl.when(c == pl.num_programs(1) - 1)
    def _():
        process(new_sc[...], True)
        o_lat = acc_sc[...] / l_sc[...]
        lane = _iota((SUBLANES, LANES), 1)
        for i in range(MLA_HEADS // 2):
            res = _dot(o_lat[i * SUBLANES:(i + 1) * SUBLANES], wuv_ref[:, i * LANES:(i + 1) * LANES])
            self_ref[0, :, i * LANES:(i + 1) * LANES] = jnp.where(lane < MLA_VD, res, pltpu.roll(res, 4, 0))


def _page_spec(page_shape, layer_j, n_pages, pps, i):
    def index(s, c, pt):
        return (pt[s * n_pages + c * pps + i], layer_j) + (0,) * len(page_shape)
    return pl.BlockSpec((1, 1) + page_shape, index)


def _mla_sample(page_table, cache, layer_j, lhsq, w_t, rows_new, w_uv, cq8, memk, memv, layer_i):
    db, n_pages = page_table.shape
    pps = min(PAGES_PER_STEP, n_pages)
    nc = n_pages // pps
    nrow = MLA_HEADS * 4
    seq = lambda shape: pl.BlockSpec((1,) + shape, lambda s, c, pt: (s,) + (0,) * len(shape))
    const = lambda shape: pl.BlockSpec(shape, lambda s, c, pt: (0,) * len(shape))
    mem = pl.BlockSpec((1, 1, MEM_LEN, MEM_W), lambda s, c, pt: (s, layer_i, 0, 0))
    grid_spec = pltpu.PrefetchScalarGridSpec(
        num_scalar_prefetch=1,
        grid=(db, nc),
        in_specs=[_page_spec((MLA_ROW, PAGE), layer_j, n_pages, pps, i) for i in range(pps)] + [
            seq((2 * nrow, ROW_PAD)), const(w_t.shape), seq((MLA_ROW, LANES)), const(w_uv.shape),
            seq((SUBLANES, MEM_W)), mem, mem],
        out_specs=[seq((SUBLANES, SELF_W)), seq((SUBLANES, MEM_W))],
        scratch_shapes=[pltpu.VMEM((MLA_HEADS * MLA_NOPE + 2 * nrow, ROW_PAD), BF16),
                        pltpu.VMEM((nrow, LANES), F32), pltpu.VMEM((nrow, LANES), F32),
                        pltpu.VMEM((nrow, LANES), F32), pltpu.VMEM((ROW_PAD, LANES), BF16)])
    return pl.pallas_call(
        functools.partial(_mla_sample_kernel, pps),
        grid_spec=grid_spec,
        out_shape=[jax.ShapeDtypeStruct((db, SUBLANES, SELF_W), F32),
                   jax.ShapeDtypeStruct((db, SUBLANES, MEM_W), F32)],
        compiler_params=_params(("arbitrary", "arbitrary")),
        name="mla_sample",
    )(page_table.reshape(-1), *([cache] * pps), lhsq, w_t, rows_new, w_uv, cq8, memk, memv)


def _dif_sample_kernel(pps, lam_init, past, pt_ref, *refs):
    kpages = refs[:pps]
    vpages = refs[pps:2 * pps]
    (lhsq_ref, knew_ref, vnew_ref, nsl_ref, lam_ref, sg_ref,
     cq_ref, mk_ref, mv_ref) = refs[2 * pps:2 * pps + 9]
    self_ref, cross_ref = refs[2 * pps + 9:2 * pps + 11]
    m_sc, l_sc, acc_sc, knew_sc, vnew_sc = refs[2 * pps + 11:]
    del pt_ref
    c = pl.program_id(1)
    nrow = DIF_KV_HEADS * 16

    @pl.when(c == 0)
    def _():
        m_sc[...] = jnp.full(m_sc.shape, NEG_INF, F32)
        l_sc[...] = jnp.zeros(l_sc.shape, F32)
        acc_sc[...] = jnp.zeros(acc_sc.shape, F32)
        knew_sc[...] = jnp.zeros(knew_sc.shape, F32)
        vnew_sc[...] = jnp.zeros(vnew_sc.shape, F32)
        knew_sc[0:SUBLANES, :] = knew_ref[0]
        vnew_sc[0:SUBLANES, :] = vnew_ref[0]
        cross_ref[0] = _sample_cross(cq_ref[0], mk_ref[0, 0], mv_ref[0, 0])

    def process(kts, vts, base, is_new):
        q = lhsq_ref[0]
        s = jnp.concatenate(
            [jnp.concatenate([_dot_nt(q[kv * 16:(kv + 1) * 16], kt) for kt in kts[kv]], axis=1)
             for kv in range(DIF_KV_HEADS)], axis=0)
        w = s.shape[1]
        qrow = _iota((nrow, w), 0) & 3
        col = _iota((nrow, w), 1)
        dist = (past + qrow - base - col).astype(F32)
        s = s + nsl_ref[:, 0:1] * dist
        if is_new:
            s = jnp.where(col <= qrow, s, NEG_INF)

        def pv(p):
            outs = []
            for kv in range(DIF_KV_HEADS):
                o = 0
                for ti, vt in enumerate(vts[kv]):
                    rows = vt.shape[0]
                    o = o + _dot(p[kv * 16:(kv + 1) * 16, ti * rows:(ti + 1) * rows], vt)
                outs.append(o)
            return jnp.concatenate(outs, axis=0)
        _softmax_update(s, m_sc, l_sc, acc_sc, pv)

    def tiles(pg):
        return [[jnp.concatenate([pg[2 * i][0, 0, kv], pg[2 * i + 1][0, 0, kv]], axis=0)
                 for i in range(pps // 2)] for kv in range(DIF_KV_HEADS)]

    process(tiles(kpages), tiles(vpages), c * pps * PAGE, False)

    @pl.when(c == pl.num_programs(1) - 1)
    def _():
        heads = lambda ref: [[ref[:, kv * LANES:(kv + 1) * LANES]] for kv in range(DIF_KV_HEADS)]
        process(heads(knew_sc), heads(vnew_sc), past, True)
        a = acc_sc[...] / l_sc[...]
        lam = _dif_lambda(lam_ref[...], lam_init)
        for i in range(DIF_HEADS):
            an = a[i * SUBLANES:(i + 1) * SUBLANES]
            o = an - lam * pltpu.roll(an, 4, 0)
            self_ref[0, :, i * LANES:(i + 1) * LANES] = _rms(o) * sg_ref[...] * (1.0 - lam_init)


def _dif_sample(page_table, cache_k, cache_v, layer_j, lhsq, k_new, v_new, nsl, lam_p, sub_g,
                lam_init, cq8, memk, memv, layer_i):
    db, n_pages = page_table.shape
    pps = min(PAGES_PER_STEP, n_pages)
    nc = n_pages // pps
    nrow = DIF_KV_HEADS * 16
    seq = lambda shape: pl.BlockSpec((1,) + shape, lambda s, c, pt: (s,) + (0,) * len(shape))
    const = lambda shape: pl.BlockSpec(shape, lambda s, c, pt: (0,) * len(shape))
    mem = pl.BlockSpec((1, 1, MEM_LEN, MEM_W), lambda s, c, pt: (s, layer_i, 0, 0))
    pages = [_page_spec((DIF_KV_HEADS, PAGE, DIF_VD), layer_j, n_pages, pps, i) for i in range(pps)]
    grid_spec = pltpu.PrefetchScalarGridSpec(
        num_scalar_prefetch=1,
        grid=(db, nc),
        in_specs=pages + pages + [
            seq((nrow, LANES)), seq((SUBLANES, DIF_KW)), seq((SUBLANES, DIF_KW)), const(nsl.shape),
            const(lam_p.shape), const(sub_g.shape), seq((SUBLANES, MEM_W)), mem, mem],
        out_specs=[seq((SUBLANES, SELF_W)), seq((SUBLANES, MEM_W))],
        scratch_shapes=[pltpu.VMEM((nrow, LANES), F32), pltpu.VMEM((nrow, LANES), F32),
                        pltpu.VMEM((nrow, LANES), F32), pltpu.VMEM((PAGE, DIF_KW), F32),
                        pltpu.VMEM((PAGE, DIF_KW), F32)])
    return pl.pallas_call(
        functools.partial(_dif_sample_kernel, pps, lam_init, n_pages * PAGE),
        grid_spec=grid_spec,
        out_shape=[jax.ShapeDtypeStruct((db, SUBLANES, SELF_W), F32),
                   jax.ShapeDtypeStruct((db, SUBLANES, MEM_W), F32)],
        compiler_params=_params(("arbitrary", "arbitrary")),
        name="dif_sample",
    )(page_table.reshape(-1), *([cache_k] * pps), *([cache_v] * pps), lhsq, k_new, v_new, nsl,
      lam_p, sub_g, cq8, memk, memv)


def _out_kernel(with_cross, x_ref, self_ref, c_ref, gs_ref, *rest):
    if with_cross:
        mk_ref, mv_ref, w_ref, y_ref = rest
        tm = x_ref.shape[0]
        cq = c_ref[...]
        mk = mk_ref[0, 0].astype(BF16)
        mv = mv_ref[0, 0].astype(BF16)
        head = jnp.right_shift(_iota((tm, MEM_W), 1), 6)
        cross = jnp.zeros((tm, MEM_W), F32)
        for hd in range(MEM_HEADS):
            s = _dot_nt(jnp.where(head == hd, cq, 0.0), mk)
            p = jnp.exp(s - jnp.max(s, axis=-1, keepdims=True))
            o = _dot(p, mv) / jnp.sum(p, axis=-1, keepdims=True)
            cross = jnp.where(head == hd, o, cross)
    else:
        w_ref, y_ref = rest
        cross = c_ref[...]
    mix = jnp.concatenate([self_ref[...], cross], axis=-1) * gs_ref[...]
    y_ref[...] = x_ref[...] + _dot(mix, w_ref[...])


def _out_proj(x, self_o, c, gs, w, memk=None, memv=None, layer_i=0, tokens_per_batch=None):
    n = x.shape[0]
    tm = min(PROJ_TM, n)
    tok = lambda wd: pl.BlockSpec((tm, wd), lambda i: (i, 0))
    ins = [x, self_o, c, gs]
    specs = [tok(D_MODEL), tok(SELF_W), tok(MEM_W), tok(MIX_W)]
    with_cross = memk is not None
    if with_cross:
        per = tokens_per_batch // tm
        mem = pl.BlockSpec((1, 1, MEM_LEN, MEM_W), lambda i: (i // per, layer_i, 0, 0))
        ins += [memk, memv]
        specs += [mem, mem]
    ins.append(w)
    specs.append(_full(w.shape))
    return pl.pallas_call(
        functools.partial(_out_kernel, with_cross),
        grid=(n // tm,),
        in_specs=specs,
        out_specs=tok(D_MODEL),
        out_shape=jax.ShapeDtypeStruct((n, D_MODEL), F32),
        compiler_params=_params(("parallel",)),
        name="out_proj_prompt" if with_cross else "out_proj_sample",
    )(*ins)


def _rope_tables(pos, g_nope, g_rope, scale):
    half = MLA_ROPE // 2
    inv_freq = ROPE_THETA ** (-jnp.arange(half, dtype=F32) / half)
    ang = pos.astype(F32)[:, None] * inv_freq
    c, s = jnp.cos(ang), jnp.sin(ang)
    g1, g2 = g_rope[:half].astype(F32), g_rope[half:].astype(F32)
    n = pos.shape[0]
    z32 = jnp.zeros((n, LANES - MLA_NOPE - MLA_ROPE), F32)
    a = jnp.concatenate([jnp.broadcast_to(g_nope.astype(F32), (n, MLA_NOPE)), g1 * c, g2 * c, z32], axis=1)
    b = jnp.concatenate([jnp.zeros((n, MLA_NOPE), F32), -g2 * s, g1 * s, z32], axis=1)
    return a * scale, b * scale


def _row(v):
    return v.astype(F32).reshape(1, -1)


def _mla_weights(i, j, norm_g, cq_norm_g, mla_w_in, mla_q_a_norm_g, mla_w_q_up, mla_kv_a_norm_g,
                 mla_w_uk, mla_w_uv, mla_kn_g):
    w = mla_w_in[j]
    c0 = Q_LORA + KV_LORA
    x1 = w[:, c0:c0 + 16]
    x2 = w[:, c0 + 16:c0 + 32]
    w_in = jnp.concatenate([w[:, :c0], jnp.zeros((D_MODEL, 64), w.dtype), x1, x2, x2, x1,
                            w[:, c0 + 32:]], axis=1).astype(BF16)
    wq = mla_w_q_up[j].reshape(Q_LORA, MLA_HEADS, MLA_NOPE + MLA_ROPE)
    qn, q1, q2 = wq[..., :MLA_NOPE], wq[..., MLA_NOPE:MLA_NOPE + 16], wq[..., MLA_NOPE + 16:]
    w_q = jnp.concatenate([qn, q1, q2, q2, q1], axis=-1).reshape(Q_LORA, MLA_HEADS * LANES).astype(BF16)
    wuk = mla_w_uk[j].reshape(KV_LORA, MLA_HEADS, MLA_NOPE)
    w_uk_pad = jnp.pad(wuk, ((0, 0), (0, 0), (0, LANES - MLA_NOPE))).reshape(KV_LORA, -1).astype(BF16)
    w_t = jnp.pad(mla_w_uk[j].T, ((0, 0), (0, ROW_PAD - KV_LORA))).astype(BF16)
    wabs = jnp.transpose(wuk * mla_kn_g[j].astype(F32), (1, 2, 0))
    w_abs = jnp.pad(wabs, ((0, 0), (0, LANES - MLA_NOPE), (0, 0))).astype(BF16)
    kn_g = _row(jnp.pad(mla_kn_g[j], (0, LANES - MLA_NOPE)))
    return dict(g=_row(norm_g[i]), w_in=w_in, q_a_g=_row(mla_q_a_norm_g[j]), w_q=w_q,
                kv_a_g=_row(mla_kv_a_norm_g[j]), w_uk_pad=w_uk_pad, w_t=w_t, w_abs=w_abs, kn_g=kn_g,
                w_uv=mla_w_uv[j].astype(BF16),
                cq_g=_row(jnp.tile(cq_norm_g[i], 2)) * MEM_HD ** -0.5)


def kernel(x_prompt, x_sample, cache_mla_kv, cache_diff_k, cache_diff_v, cache_mem_k, cache_mem_v,
           page_table, mem_prompt, norm_g, w_out, mem_norm_g, w_mem_kv, cq_norm_g, ck_norm_g,
           mla_w_in, mla_q_a_norm_g, mla_w_q_up, mla_kv_a_norm_g, mla_w_uk, mla_w_uv,
           mla_qn_g, mla_kn_g, mla_qr_g, mla_kr_g,
           dif_w_in, dif_q_g, dif_k_g, dif_lambda, dif_subln_g):
    b, t, d = x_prompt.shape
    db, ds, _ = x_sample.shape
    n_pages = page_table.shape[1]
    past = n_pages * PAGE
    depth = norm_g.shape[0]
    assert d == D_MODEL and ds == 4 and t % max(PROJ_TM, MLA_FLASH_T, DIF_FLASH_T) == 0
    assert n_pages % min(PAGES_PER_STEP, n_pages) == 0 and min(PAGES_PER_STEP, n_pages) % 2 == 0

    pos_p = jnp.arange(t, dtype=jnp.int32)
    pos_s = jnp.tile(past + jnp.arange(ds, dtype=jnp.int32), db)
    xp = x_prompt.reshape(b * t, d)
    xs = x_sample.reshape(db * ds, d)
    w_out_b = w_out.astype(BF16)

    memk, memv = _mem_kv(mem_prompt, mem_norm_g.astype(F32).reshape(depth, 1, d), w_mem_kv.astype(BF16),
                         jnp.tile(ck_norm_g.astype(F32), (1, 2)).reshape(depth, 1, LANES))
    smem_k = jnp.transpose(cache_mem_k, (0, 1, 3, 4, 2)).reshape(db, depth, MEM_W, MEM_LEN)
    smem_v = jnp.transpose(cache_mem_v, (0, 1, 3, 4, 2)).reshape(db, depth, MEM_W, MEM_LEN)
    cache_mla_t = jnp.swapaxes(cache_mla_kv, 2, 3)
    cache_dk = jnp.transpose(cache_diff_k, (0, 1, 3, 2, 4))
    cache_dv = jnp.transpose(cache_diff_v, (0, 1, 3, 2, 4))

    def pad8(a, k=1):
        return jnp.pad(a.reshape(db, ds, -1), ((0, 0), (0, k * SUBLANES - ds), (0, 0)))

    mla_rows_p, mla_rows_s, dk_p, dv_p, dk_s, dv_s = [], [], [], [], [], []
    slopes = 2.0 ** (-8.0 * jnp.arange(1, DIF_HEADS + 1, dtype=F32) / DIF_HEADS)

    for i in range(depth):
        j = i // 2
        if i % 2 == 0:
            lw = _mla_weights(i, j, norm_g, cq_norm_g, mla_w_in, mla_q_a_norm_g, mla_w_q_up,
                              mla_kv_a_norm_g, mla_w_uk, mla_w_uv, mla_kn_g)
            scale = (MLA_NOPE + MLA_ROPE) ** -0.5
            tabs_p = (*_rope_tables(pos_p, mla_qn_g[j], mla_qr_g[j], scale),
                      *_rope_tables(pos_p, jnp.zeros((MLA_NOPE,), F32), mla_kr_g[j], 1.0))
            tabs_s = (*_rope_tables(pos_s, mla_qn_g[j], mla_qr_g[j], scale),
                      *_rope_tables(pos_s, jnp.zeros((MLA_NOPE,), F32), mla_kr_g[j], 1.0))
            qhat, khat, v_p, rows_p, cq_p, gs_p = _mla_proj(xp, tabs_p, lw, False)
            qabs, qrope, rows_s, cq_s, gs_s = _mla_proj(xs, tabs_s, lw, True)
            self_p = _mla_flash(qhat, khat, v_p, b, t)

            def to_rows(a):
                return a.reshape(db, ds, MLA_HEADS, LANES).transpose(0, 2, 1, 3).reshape(db, MLA_HEADS * ds, LANES)
            lhs_n = jnp.pad(to_rows(qabs), ((0, 0), (0, 0), (0, ROW_PAD - KV_LORA)))
            lhs_r = jnp.pad(to_rows(qrope), ((0, 0), (0, 0), (KV_LORA, 0)))
            lhsq = jnp.concatenate([lhs_n, lhs_r], axis=1)
            new_t = jnp.pad(jnp.swapaxes(rows_s.reshape(db, ds, MLA_ROW), 1, 2),
                            ((0, 0), (0, 0), (0, LANES - ds)))
            self_s, cross_s = _mla_sample(page_table, cache_mla_t, j, lhsq, lw["w_t"], new_t,
                                          lw["w_uv"], pad8(cq_s), smem_k, smem_v, i)
            mla_rows_p.append(rows_p.reshape(b, t, MLA_ROW))
            mla_rows_s.append(rows_s.reshape(db, ds, MLA_ROW))
        else:
            lam_init = 0.8 - 0.6 * math.exp(-0.3 * i)
            lw = dict(g=_row(norm_g[i]), w_in=dif_w_in[j].astype(BF16),
                      q_g=_row(jnp.tile(dif_q_g[j], 2)) * DIF_HD ** -0.5, k_g=_row(jnp.tile(dif_k_g[j], 2)),
                      cq_g=_row(jnp.tile(cq_norm_g[i], 2)) * MEM_HD ** -0.5)
            lam_p = dif_lambda[j].astype(F32)
            sub_g = _row(dif_subln_g[j])
            qn_p, kn_p, v_p, cq_p, gs_p = _dif_proj(xp, lw, b)
            qn_s, kn_s, v_s, cq_s, gs_s = _dif_proj(xs, lw, 1)
            to_tok = lambda a: jnp.transpose(a[0], (1, 0, 2)).reshape(db * ds, DIF_KW)
            kn_s, v_s = to_tok(kn_s), to_tok(v_s)
            slope_tab = slopes.reshape(DIF_KV_HEADS, 2)
            self_p = _dif_flash(qn_p, kn_p, v_p, slope_tab, lam_p, sub_g, lam_init, b, t)

            q5 = qn_s.reshape(db, ds, DIF_KV_HEADS, 2, 1, LANES).transpose(0, 2, 3, 4, 1, 5)
            lane_map = (jnp.arange(LANES) // DIF_HD)[None, None, None, None, None, :]
            q6 = jnp.where(lane_map == jnp.arange(2)[None, None, None, :, None, None], q5, 0.0)
            lhsq = q6.reshape(db, DIF_KV_HEADS * 16, LANES)
            nsl = jnp.broadcast_to(-jnp.repeat(slopes, 8)[:, None], (DIF_KV_HEADS * 16, LANES))
            self_s, cross_s = _dif_sample(page_table, cache_dk, cache_dv, j, lhsq, pad8(kn_s), pad8(v_s),
                                          nsl, lam_p, sub_g, lam_init, pad8(cq_s), smem_k, smem_v, i)
            dk_p.append(kn_p)
            dv_p.append(v_p)
            dk_s.append(kn_s.reshape(db, ds, DIF_KV_HEADS, DIF_VD))
            dv_s.append(v_s.reshape(db, ds, DIF_KV_HEADS, DIF_VD))

        xp = _out_proj(xp, self_p, cq_p, gs_p, w_out_b[i], memk, memv, i, t)
        xs = _out_proj(xs, self_s[:, :ds].reshape(db * ds, SELF_W), cross_s[:, :ds].reshape(db * ds, MEM_W),
                       gs_s, w_out_b[i])

    mem_shape = (b, depth, MEM_LEN, MEM_HEADS, MEM_HD)
    tok_major = lambda parts: jnp.transpose(jnp.stack(parts, axis=1), (0, 1, 3, 2, 4))
    return (xp.reshape(b, t, d), xs.reshape(db, ds, d),
            jnp.stack(mla_rows_p, axis=1), tok_major(dk_p), tok_major(dv_p),
            memk.reshape(mem_shape), memv.reshape(mem_shape),
            jnp.stack(mla_rows_s, axis=1), jnp.stack(dk_s, axis=1), jnp.stack(dv_s, axis=1))
```
